```python
import math
import jax, jax.numpy as jnp
from jax import lax
import numpy as np

D_MODEL = 4096
BATCH = 4
SEQ = 2048
DEPTH = 4
DEC_BATCH = 8
DEC_SEQ = 1
PAST_LEN = 8192
PAGE_SIZE = 128

HEAD_DIM = 128
N_HEADS_A = D_MODEL // (2 * HEAD_DIM)
N_HEADS_B = D_MODEL // (2 * HEAD_DIM)
ROT_DIM = HEAD_DIM // 4
ROPE_THETA = 500000.0
MOBA_BLOCK = 256
MOBA_TOPK = 3
Q_BLOCK = 128
CONV_W = 31
MEM_LEN = 256
X_HEADS = 4
X_HEAD_DIM = 128
D_FF = -(-8 * D_MODEL // (3 * 256)) * 256
N_EVEN = (DEPTH + 1) // 2
N_ODD = DEPTH // 2
EPS = 1e-6

kernel_name = "moba_stickbreak_conformer_hybrid_step"


def _rmsnorm(x, g):
    xf = x.astype(jnp.float32)
    y = xf * lax.rsqrt(jnp.mean(xf * xf, -1, keepdims=True) + EPS)
    return (y * g.astype(jnp.float32)).astype(x.dtype)


def _layernorm(x, g, b):
    xf = x.astype(jnp.float32)
    mu = jnp.mean(xf, -1, keepdims=True)
    xc = xf - mu
    y = xc * lax.rsqrt(jnp.mean(xc * xc, -1, keepdims=True) + EPS)
    return (y * g.astype(jnp.float32) + b.astype(jnp.float32)).astype(x.dtype)


def _rotary(x, pos):
    half = ROT_DIM // 2
    inv = ROPE_THETA ** (-jnp.arange(0, ROT_DIM, 2, dtype=jnp.float32) / ROT_DIM)
    ang = pos.astype(jnp.float32)[:, None] * inv[None, :]
    cos = jnp.cos(ang)[None, :, None, :]
    sin = jnp.sin(ang)[None, :, None, :]
    xr = x[..., :ROT_DIM].astype(jnp.float32)
    x1, x2 = xr[..., :half], xr[..., half:]
    rot = jnp.concatenate([x1 * cos - x2 * sin, x2 * cos + x1 * sin], -1)
    return jnp.concatenate([rot.astype(x.dtype), x[..., ROT_DIM:]], -1)


def _moba(q, k, v, q_pos):
    n, tq, h, dh = q.shape
    l = k.shape[1]
    nb = max(-(-l // MOBA_BLOCK), MOBA_TOPK + 1)
    pad = nb * MOBA_BLOCK - l
    k = jnp.pad(k, ((0, 0), (0, pad), (0, 0), (0, 0)))
    v = jnp.pad(v, ((0, 0), (0, pad), (0, 0), (0, 0)))
    qr = q.transpose(0, 2, 1, 3).reshape(n * h, tq, dh)
    kr = k.transpose(0, 2, 1, 3).reshape(n * h, nb, MOBA_BLOCK, dh)
    vr = v.transpose(0, 2, 1, 3).reshape(n * h, nb, MOBA_BLOCK, dh)
    qb = math.gcd(tq, Q_BLOCK)
    nc = tq // qb
    pos_c = q_pos.reshape(nc, qb)
    scale = dh ** -0.5
    blk_ids = jnp.arange(nb)
    slot_ids = jnp.arange(MOBA_TOPK)
    offs = jnp.arange(MOBA_BLOCK)

    def row(args):
        qx, kx, vx = args
        kmean = jnp.mean(kx.astype(jnp.float32), axis=1)

        def chunk(cargs):
            qq, pp = cargs
            own = pp // MOBA_BLOCK
            gate = jnp.einsum('qd,bd->qb', qq.astype(jnp.float32), kmean)
            gate = jnp.where(blk_ids[None, :] < own[:, None], gate, -jnp.inf)
            _, top = lax.top_k(gate, MOBA_TOPK)
            sel = jnp.concatenate([top, own[:, None]], 1)
            kg = kx[sel]
            vg = vx[sel]
            s = jnp.einsum('qd,qjkd->qjk', qq, kg, preferred_element_type=jnp.float32) * scale
            kpos = sel[..., None] * MOBA_BLOCK + offs
            slot_ok = jnp.concatenate([slot_ids[None, :] < own[:, None],
                                       jnp.ones((qb, 1), bool)], 1)
            ok = slot_ok[..., None] & (kpos <= pp[:, None, None])
            s = jnp.where(ok, s, -jnp.inf)
            p = jax.nn.softmax(s.reshape(qb, -1), axis=-1).reshape(s.shape)
            return jnp.einsum('qjk,qjkd->qd', p, vg).astype(q.dtype)

        out = lax.map(chunk, (qx.reshape(nc, qb, dh), pos_c))
        return out.reshape(tq, dh)

    o = lax.map(row, (qr, kr, vr))
    return o.reshape(n, h, tq, dh).transpose(0, 2, 1, 3)


def _stick_breaking(q, k, v, q_pos):
    n, tq, h, dh = q.shape
    l = k.shape[1]
    qb = math.gcd(tq, Q_BLOCK)
    nc = tq // qb
    k_pos = jnp.arange(l)
    qc = q.reshape(n, nc, qb, h, dh).transpose(1, 0, 2, 3, 4)
    pc = q_pos.reshape(nc, qb)
    scale = dh ** -0.5

    def chunk(args):
        qq, pp = args
        z = jnp.einsum('nqhd,nkhd->nhqk', qq, k, preferred_element_type=jnp.float32) * scale
        past = (k_pos[None, :] < pp[:, None])[None, None]
        log_keep = jnp.where(past, jax.nn.log_sigmoid(-z), 0.0)
        after = lax.cumsum(log_keep, axis=3, reverse=True) - log_keep
        a = jnp.where(past, jnp.exp(jax.nn.log_sigmoid(z) + after), 0.0)
        return jnp.einsum('nhqk,nkhd->nqhd', a, v).astype(q.dtype)

    o = lax.map(chunk, (qc, pc))
    return o.transpose(1, 0, 2, 3, 4).reshape(n, tq, h, dh)


def _conv_module(h, prefix, w_pw1, w_dw, b_dw, g_ln, b_ln, w_pw2):
    u = h @ w_pw1
    a, g = jnp.split(u, 2, axis=-1)
    u = a * jax.nn.sigmoid(g)
    up = jnp.concatenate([prefix, u], 1)
    y = lax.conv_general_dilated(up, w_dw[:, None, :], window_strides=(1,), padding='VALID',
                                 dimension_numbers=('NWC', 'WIO', 'NWC'),
                                 feature_group_count=up.shape[-1])
    y = jax.nn.silu(_layernorm(y + b_dw, g_ln, b_ln))
    return y @ w_pw2, up[:, -(CONV_W - 1):]


def _cross_attn(h, mk, mv, w_q, w_o):
    n, t, _ = h.shape
    q = (h @ w_q).reshape(n, t, X_HEADS, X_HEAD_DIM)
    s = jnp.einsum('nqhd,nmhd->nhqm', q, mk, preferred_element_type=jnp.float32) * (X_HEAD_DIM ** -0.5)
    p = jax.nn.softmax(s, axis=-1)
    o = jnp.einsum('nhqm,nmhd->nqhd', p, mv).astype(h.dtype).reshape(n, t, X_HEADS * X_HEAD_DIM)
    return o @ w_o


def _ffn(h, wg, wu, wd):
    return (jax.nn.silu(h @ wg) * (h @ wu)) @ wd


def _paged_rows(cache, table):
    g = cache[table]
    return g.reshape(table.shape[0], -1, cache.shape[2], cache.shape[3])


def _trunk(x, pos, w, mem=None, mem_k=None, mem_v=None, paged=None, conv_state=None):
    n, t, _ = x.shape
    hd_a = N_HEADS_A * HEAD_DIM
    hd_b = N_HEADS_B * HEAD_DIM
    nka, nva, nkb, nvb, nmk, nmv, nconv = [], [], [], [], [], [], []
    for l in range(DEPTH):
        h = _rmsnorm(x, w['g_mix'][l])
        if l % 2 == 0:
            i = l // 2
            proj = h @ w['w_in'][i]
            qa, ka, va = [proj[..., j * hd_a:(j + 1) * hd_a].reshape(n, t, N_HEADS_A, HEAD_DIM)
                          for j in range(3)]
            off = 3 * hd_a
            qb, kb, vb = [proj[..., off + j * hd_b:off + (j + 1) * hd_b].reshape(n, t, N_HEADS_B, HEAD_DIM)
                          for j in range(3)]
            qa = _rotary(qa, pos)
            ka = _rotary(ka, pos)
            nka.append(ka); nva.append(va); nkb.append(kb); nvb.append(vb)
            if paged is None:
                fka, fva, fkb, fvb = ka, va, kb, vb
            else:
                ck_a, cv_a, ck_b, cv_b, table = paged
                fka = jnp.concatenate([_paged_rows(ck_a[i], table), ka], 1)
                fva = jnp.concatenate([_paged_rows(cv_a[i], table), va], 1)
                fkb = jnp.concatenate([_paged_rows(ck_b[i], table), kb], 1)
                fvb = jnp.concatenate([_paged_rows(cv_b[i], table), vb], 1)
            oa = _moba(qa, fka, fva, pos).reshape(n, t, hd_a)
            ob = _stick_breaking(qb, fkb, fvb, pos).reshape(n, t, hd_b)
            x = x + jnp.concatenate([oa, ob], -1) @ w['w_out'][i]
        else:
            j = l // 2
            prefix = (jnp.zeros((n, CONV_W - 1, x.shape[-1]), x.dtype)
                      if conv_state is None else conv_state[j])
            o, st = _conv_module(h, prefix, w['w_pw1'][j], w['w_dw'][j], w['b_dw'][j],
                                 w['g_conv_ln'][j], w['b_conv_ln'][j], w['w_pw2'][j])
            x = x + o
            nconv.append(st)
        if mem is not None:
            m = _rmsnorm(mem, w['g_mem'][l])
            kv = (m @ w['w_xkv'][l]).reshape(n, mem.shape[1], 2, X_HEADS, X_HEAD_DIM)
            mk, mv = kv[:, :, 0], kv[:, :, 1]
            nmk.append(mk); nmv.append(mv)
        else:
            mk, mv = mem_k[l], mem_v[l]
        x = x + _cross_attn(_rmsnorm(x, w['g_xattn'][l]), mk, mv, w['w_xq'][l], w['w_xo'][l])
        x = x + _ffn(_rmsnorm(x, w['g_ffn'][l]), w['w_ffn_gate'][l], w['w_ffn_up'][l], w['w_ffn_down'][l])
    y = _rmsnorm(x, w['g_final'])
    return y, nka, nva, nkb, nvb, nmk, nmv, nconv


def setup_inputs(seed: int = 0) -> dict:
    key = jax.random.key(seed)
    ks = jax.random.split(key, 40)
    kit = iter([ks[i] for i in range(40)])
    f32 = jnp.float32

    def nrm(shape, scale=1.0):
        return jax.random.normal(next(kit), shape, f32) * scale

    def gain(shape):
        return 1.0 + nrm(shape, 0.01)

    n_pages = PAST_LEN // PAGE_SIZE
    n_used = DEC_BATCH * n_pages
    n_phys = n_used + max(n_used // 4, 1)
    hd_a = N_HEADS_A * HEAD_DIM
    hd_b = N_HEADS_B * HEAD_DIM
    xw = X_HEADS * X_HEAD_DIM
    page_table = jax.random.permutation(next(kit), n_phys)[:n_used].reshape(DEC_BATCH, n_pages).astype(jnp.int32)
    return {
        'x_prompt': nrm((BATCH, SEQ, D_MODEL)),
        'x_sample': nrm((DEC_BATCH, DEC_SEQ, D_MODEL)),
        'cache_k_a': nrm((N_EVEN, n_phys, PAGE_SIZE, N_HEADS_A, HEAD_DIM)),
        'cache_v_a': nrm((N_EVEN, n_phys, PAGE_SIZE, N_HEADS_A, HEAD_DIM)),
        'cache_k_b': nrm((N_EVEN, n_phys, PAGE_SIZE, N_HEADS_B, HEAD_DIM)),
        'cache_v_b': nrm((N_EVEN, n_phys, PAGE_SIZE, N_HEADS_B, HEAD_DIM)),
        'cache_mem_k': nrm((DEPTH, DEC_BATCH, MEM_LEN, X_HEADS, X_HEAD_DIM)),
        'cache_mem_v': nrm((DEPTH, DEC_BATCH, MEM_LEN, X_HEADS, X_HEAD_DIM)),
        'state_conv': nrm((N_ODD, DEC_BATCH, CONV_W - 1, D_MODEL)),
        'page_table': page_table,
        'mem_prompt': nrm((BATCH, MEM_LEN, D_MODEL)),
        'g_mix': gain((DEPTH, D_MODEL)),
        'w_in': nrm((N_EVEN, D_MODEL, 3 * hd_a + 3 * hd_b), D_MODEL ** -0.5),
        'w_out': nrm((N_EVEN, hd_a + hd_b, D_MODEL), (hd_a + hd_b) ** -0.5),
        'w_pw1': nrm((N_ODD, D_MODEL, 2 * D_MODEL), D_MODEL ** -0.5),
        'w_dw': nrm((N_ODD, CONV_W, D_MODEL), CONV_W ** -0.5),
        'b_dw': nrm((N_ODD, D_MODEL), 0.01),
        'g_conv_ln': gain((N_ODD, D_MODEL)),
        'b_conv_ln': nrm((N_ODD, D_MODEL), 0.01),
        'w_pw2': nrm((N_ODD, D_MODEL, D_MODEL), D_MODEL ** -0.5),
        'g_mem': gain((DEPTH, D_MODEL)),
        'g_xattn': gain((DEPTH, D_MODEL)),
        'w_xq': nrm((DEPTH, D_MODEL, xw), D_MODEL ** -0.5),
        'w_xkv': nrm((DEPTH, D_MODEL, 2 * xw), D_MODEL ** -0.5),
        'w_xo': nrm((DEPTH, xw, D_MODEL), xw ** -0.5),
        'g_ffn': gain((DEPTH, D_MODEL)),
        'w_ffn_gate': nrm((DEPTH, D_MODEL, D_FF), D_MODEL ** -0.5),
        'w_ffn_up': nrm((DEPTH, D_MODEL, D_FF), D_MODEL ** -0.5),
        'w_ffn_down': nrm((DEPTH, D_FF, D_MODEL), D_FF ** -0.5),
        'g_final': gain((D_MODEL,)),
    }


def reference(x_prompt, x_sample, cache_k_a, cache_v_a, cache_k_b, cache_v_b, cache_mem_k, cache_mem_v,
              state_conv, page_table, mem_prompt, g_mix, w_in, w_out, w_pw1, w_dw, b_dw, g_conv_ln,
              b_conv_ln, w_pw2, g_mem, g_xattn, w_xq, w_xkv, w_xo, g_ffn, w_ffn_gate, w_ffn_up,
              w_ffn_down, g_final):
    w = {'g_mix': g_mix, 'w_in': w_in, 'w_out': w_out, 'w_pw1': w_pw1, 'w_dw': w_dw, 'b_dw': b_dw,
         'g_conv_ln': g_conv_ln, 'b_conv_ln': b_conv_ln, 'w_pw2': w_pw2, 'g_mem': g_mem,
         'g_xattn': g_xattn, 'w_xq': w_xq, 'w_xkv': w_xkv, 'w_xo': w_xo, 'g_ffn': g_ffn,
         'w_ffn_gate': w_ffn_gate, 'w_ffn_up': w_ffn_up, 'w_ffn_down': w_ffn_down, 'g_final': g_final}
    pos_p = jnp.arange(x_prompt.shape[1], dtype=jnp.int32)
    pos_s = PAST_LEN + jnp.arange(x_sample.shape[1], dtype=jnp.int32)
    y_prompt, ka_p, va_p, kb_p, vb_p, mk_p, mv_p, cv_p = _trunk(x_prompt, pos_p, w, mem=mem_prompt)
    y_sample, ka_s, va_s, kb_s, vb_s, _, _, cv_s = _trunk(
        x_sample, pos_s, w, mem_k=cache_mem_k, mem_v=cache_mem_v,
        paged=(cache_k_a, cache_v_a, cache_k_b, cache_v_b, page_table), conv_state=state_conv)
    return (y_prompt, y_sample,
            jnp.stack(ka_p), jnp.stack(va_p), jnp.stack(kb_p), jnp.stack(vb_p),
            jnp.stack(mk_p), jnp.stack(mv_p), jnp.stack(cv_p),
            jnp.stack(ka_s), jnp.stack(va_s), jnp.stack(kb_s), jnp.stack(vb_s), jnp.stack(cv_s))
```

```python
import functools
import math

import jax
import jax.numpy as jnp
from jax import lax
from jax.experimental import pallas as pl
from jax.experimental.pallas import tpu as pltpu

F32 = jnp.float32
BF16 = jnp.bfloat16

HEAD_DIM = 128
ROT_DIM = HEAD_DIM // 4
ROPE_THETA = 500000.0
MOBA_BLOCK = 256
MOBA_TOPK = 3
CONV_W = 31
X_HEADS = 4
X_HEAD_DIM = 128
EPS = 1e-6
PAGE_SIZE = 128

V7X_VMEM_LIMIT_BYTES = 56 * 1024 * 1024
NEG_BIG = -1e30
CONV_PAD = 32


def _cparams(*sem):
    return pltpu.CompilerParams(dimension_semantics=sem, vmem_limit_bytes=V7X_VMEM_LIMIT_BYTES)


def _sigmoid(x):
    return 1.0 / (1.0 + jnp.exp(-x))


def _rmsnorm_kernel(x_ref, g_ref, o_ref):
    x = x_ref[...]
    ms = jnp.mean(x * x, axis=-1, keepdims=True)
    o_ref[...] = (x * lax.rsqrt(ms + EPS) * g_ref[...]).astype(o_ref.dtype)


def _rmsnorm(x, g, out_dtype, tm=256):
    m, d = x.shape
    tm = min(tm, m)
    return pl.pallas_call(
        _rmsnorm_kernel,
        grid=(m // tm,),
        in_specs=[pl.BlockSpec((tm, d), lambda i: (i, 0)),
                  pl.BlockSpec((1, d), lambda i: (0, 0))],
        out_specs=pl.BlockSpec((tm, d), lambda i: (i, 0)),
        out_shape=jax.ShapeDtypeStruct((m, d), out_dtype),
        compiler_params=_cparams("parallel"),
        name="rmsnorm",
    )(x, g.reshape(1, d))


def _mm_kernel(*refs, nk, n_w, has_res, epilogue):
    a_ref = refs[0]
    w_refs = refs[1:1 + n_w]
    pos = 1 + n_w
    r_ref = refs[pos] if has_res else None
    pos += int(has_res)
    o_ref = refs[pos]
    acc_refs = refs[pos + 1:]

    a = a_ref[...].astype(BF16)
    ps = [jnp.dot(a, w_ref[...].astype(BF16), preferred_element_type=F32) for w_ref in w_refs]

    def finish(vals):
        if epilogue == "swiglu":
            y = vals[0] * _sigmoid(vals[0]) * vals[1]
        elif epilogue == "glu":
            y = vals[0] * _sigmoid(vals[1])
        else:
            y = vals[0]
        if has_res:
            y = r_ref[...] + y
        o_ref[...] = y.astype(o_ref.dtype)

    if nk == 1:
        finish(ps)
    else:
        kk = pl.program_id(2)

        @pl.when(kk == 0)
        def _():
            for acc, p in zip(acc_refs, ps):
                acc[...] = p

        @pl.when(kk > 0)
        def _():
            for acc, p in zip(acc_refs, ps):
                acc[...] += p

        @pl.when(kk == nk - 1)
        def _():
            finish([acc[...] for acc in acc_refs])


def _mm(a, w, *, n, k=None, col_off=0, row_off=0, w2=None, col_off2=0, epilogue="none",
        residual=None, out_dtype=F32, tm=1024, tn=512, tk=None):
    m, ka = a.shape
    k = ka if k is None else k
    tm = min(tm, m)
    tn = min(tn, n)
    tk = k if tk is None else tk
    nk = k // tk
    assert m % tm == 0 and n % tn == 0 and k % tk == 0
    assert col_off % tn == 0 and col_off2 % tn == 0 and row_off % tk == 0
    cb, cb2, rb = col_off // tn, col_off2 // tn, row_off // tk
    ws = [w] if w2 is None else [w, w2]
    cbs = [cb, cb2]
    in_specs = [pl.BlockSpec((tm, tk), lambda i, j, kk: (i, kk))]
    for idx in range(len(ws)):
        in_specs.append(pl.BlockSpec((tk, tn), functools.partial(
            lambda i, j, kk, c: (rb + kk, c + j), c=cbs[idx])))
    args = [a] + ws
    if residual is not None:
        in_specs.append(pl.BlockSpec((tm, tn), lambda i, j, kk: (i, j)))
        args.append(residual)
    scratch = [pltpu.VMEM((tm, tn), F32) for _ in ws] if nk > 1 else []
    return pl.pallas_call(
        functools.partial(_mm_kernel, nk=nk, n_w=len(ws), has_res=residual is not None,
                          epilogue=epilogue),
        grid=(m // tm, n // tn, nk),
        in_specs=in_specs,
        out_specs=pl.BlockSpec((tm, tn), lambda i, j, kk: (i, j)),
        out_shape=jax.ShapeDtypeStruct((m, n), out_dtype),
        scratch_shapes=scratch,
        compiler_params=_cparams("parallel", "parallel", "arbitrary"),
        name="mm_" + epilogue,
    )(*args)


def _rope_tables(pos):
    half = ROT_DIM // 2
    inv = ROPE_THETA ** (-jnp.arange(0, ROT_DIM, 2, dtype=F32) / ROT_DIM)
    ang = pos.astype(F32)[:, None] * inv[None, :]
    cos, sin = jnp.cos(ang), jnp.sin(ang)
    t = pos.shape[0]
    rest = HEAD_DIM - ROT_DIM
    c = jnp.concatenate([cos, cos, jnp.ones((t, rest), F32)], 1)
    s1 = jnp.concatenate([-sin, jnp.zeros((t, half + rest), F32)], 1)
    s2 = jnp.concatenate([jnp.zeros((t, half), F32), sin, jnp.zeros((t, rest), F32)], 1)
    return c, s1, s2


def _rotary_kernel(x_ref, c_ref, s1_ref, s2_ref, o_ref, *, n_heads):
    half = ROT_DIM // 2
    c, s1, s2 = c_ref[...], s1_ref[...], s2_ref[...]
    for h in range(n_heads):
        sl = slice(h * HEAD_DIM, (h + 1) * HEAD_DIM)
        x = x_ref[:, sl]
        y = x * c + pltpu.roll(x, HEAD_DIM - half, 1) * s1 + pltpu.roll(x, half, 1) * s2
        o_ref[:, sl] = y.astype(o_ref.dtype)


def _rotary(x, tables, t_len, tm=256):
    m, w = x.shape
    tm = min(tm, t_len, m)
    nt = t_len // tm
    tab_spec = pl.BlockSpec((tm, HEAD_DIM), lambda i: (i % nt, 0))
    return pl.pallas_call(
        functools.partial(_rotary_kernel, n_heads=w // HEAD_DIM),
        grid=(m // tm,),
        in_specs=[pl.BlockSpec((tm, w), lambda i: (i, 0)), tab_spec, tab_spec, tab_spec],
        out_specs=pl.BlockSpec((tm, w), lambda i: (i, 0)),
        out_shape=jax.ShapeDtypeStruct((m, w), F32),
        compiler_params=_cparams("parallel"),
        name="rotary",
    )(x, *tables)


def _moba_kernel(q_ref, k_ref, v_ref, o_ref, m_ref, l_ref, acc_ref, *, nb):
    blk = MOBA_BLOCK
    qi = pl.program_id(2)
    scale = HEAD_DIM ** -0.5
    q = q_ref[...]
    kmean = jnp.mean(k_ref[...].reshape(nb, blk, HEAD_DIM), axis=1)
    gate = lax.dot_general(q, kmean, (((1,), (1,)), ((), ())),
                           precision=lax.Precision.HIGHEST, preferred_element_type=F32)
    colb = lax.broadcasted_iota(jnp.int32, (blk, nb), 1)
    valid = colb < qi
    gm = jnp.where(valid, gate, -jnp.inf)
    rank = jnp.zeros((blk, nb), jnp.int32)
    for b2 in range(nb):
        gb = gm[:, b2:b2 + 1]
        rank = rank + ((gb > gm) | ((gb == gm) & (b2 < colb))).astype(jnp.int32)
    sel = ((valid & (rank < MOBA_TOPK)) | (colb == qi)).astype(F32)

    q16 = q.astype(BF16)
    row = lax.broadcasted_iota(jnp.int32, (blk, blk), 0)
    col = lax.broadcasted_iota(jnp.int32, (blk, blk), 1)
    m_ref[...] = jnp.full((blk, 1), NEG_BIG, F32)
    l_ref[...] = jnp.zeros((blk, 1), F32)
    acc_ref[...] = jnp.zeros((blk, HEAD_DIM), F32)

    def body(kb, carry):
        off = pl.multiple_of(kb * blk, blk)
        kblk = k_ref[pl.ds(off, blk), :].astype(BF16)
        vblk = v_ref[pl.ds(off, blk), :].astype(BF16)
        s = lax.dot_general(q16, kblk, (((1,), (1,)), ((), ())), preferred_element_type=F32) * scale
        selcol = jnp.sum(jnp.where(colb == kb, sel, 0.0), axis=1, keepdims=True) > 0.5
        ok = selcol & ((kb < qi) | (col <= row))
        s = jnp.where(ok, s, NEG_BIG)
        m_prev = m_ref[...]
        m_new = jnp.maximum(m_prev, jnp.max(s, axis=1, keepdims=True))
        alpha = jnp.exp(m_prev - m_new)
        p = jnp.where(ok, jnp.exp(s - m_new), 0.0)
        l_ref[...] = alpha * l_ref[...] + jnp.sum(p, axis=1, keepdims=True)
        acc_ref[...] = alpha * acc_ref[...] + jnp.dot(p.astype(BF16), vblk, preferred_element_type=F32)
        m_ref[...] = m_new
        return carry

    lax.fori_loop(0, qi + 1, body, 0)
    o_ref[...] = (acc_ref[...] / l_ref[...]).astype(o_ref.dtype)


def _moba_prompt(q, k, v):
    b, t, w = q.shape
    h = w // HEAD_DIM
    blk = MOBA_BLOCK
    nb = max(-(-t // blk), MOBA_TOPK + 1)
    assert t % blk == 0 and nb * blk == t
    kv_spec = pl.BlockSpec((None, t, HEAD_DIM), lambda bi, hi, qi: (bi, 0, hi))
    q_spec = pl.BlockSpec((None, blk, HEAD_DIM), lambda bi, hi, qi: (bi, qi, hi))
    return pl.pallas_call(
        functools.partial(_moba_kernel, nb=nb),
        grid=(b, h, t // blk),
        in_specs=[q_spec, kv_spec, kv_spec],
        out_specs=q_spec,
        out_shape=jax.ShapeDtypeStruct((b, t, w), BF16),
        scratch_shapes=[pltpu.VMEM((blk, 1), F32), pltpu.VMEM((blk, 1), F32),
                        pltpu.VMEM((blk, HEAD_DIM), F32)],
        compiler_params=_cparams("parallel", "parallel", "arbitrary"),
        name="moba_prompt",
    )(q, k, v)


SB_BLOCK = 256


def _log_sigmoid_pair(z):
    soft = jnp.log(1.0 + jnp.exp(-jnp.abs(z)))
    ls_pos = jnp.minimum(z, 0.0) - soft
    return ls_pos, ls_pos - z


def _sb_kernel(q_ref, k_ref, v_ref, o_ref, c_ref, acc_ref):
    blk = SB_BLOCK
    qi = pl.program_id(2)
    scale = HEAD_DIM ** -0.5
    q16 = q_ref[...].astype(BF16)
    row = lax.broadcasted_iota(jnp.int32, (blk, blk), 0)
    col = lax.broadcasted_iota(jnp.int32, (blk, blk), 1)
    later = (row > col).astype(BF16)
    c_ref[...] = jnp.zeros((blk, 1), F32)
    acc_ref[...] = jnp.zeros((blk, HEAD_DIM), F32)

    def body(step, carry):
        kb = qi - step
        off = pl.multiple_of(kb * blk, blk)
        kblk = k_ref[pl.ds(off, blk), :].astype(BF16)
        vblk = v_ref[pl.ds(off, blk), :].astype(BF16)
        z = lax.dot_general(q16, kblk, (((1,), (1,)), ((), ())), preferred_element_type=F32) * scale
        past = (kb < qi) | (col < row)
        ls_pos, ls_neg = _log_sigmoid_pair(z)
        lk = jnp.where(past, ls_neg, 0.0)
        lk_hi = lk.astype(BF16)
        lk_lo = (lk - lk_hi.astype(F32)).astype(BF16)
        after = (jnp.dot(lk_hi, later, preferred_element_type=F32)
                 + jnp.dot(lk_lo, later, preferred_element_type=F32))
        a = jnp.where(past, jnp.exp(ls_pos + after + c_ref[...]), 0.0)
        acc_ref[...] += jnp.dot(a.astype(BF16), vblk, preferred_element_type=F32)
        c_ref[...] += jnp.sum(lk, axis=1, keepdims=True)
        return carry

    lax.fori_loop(0, qi + 1, body, 0)
    o_ref[...] = acc_ref[...].astype(o_ref.dtype)


def _sb_prompt(q, k, v):
    b, t, w = q.shape
    h = w // HEAD_DIM
    blk = SB_BLOCK
    assert t % blk == 0
    kv_spec = pl.BlockSpec((None, t, HEAD_DIM), lambda bi, hi, qi: (bi, 0, hi))
    q_spec = pl.BlockSpec((None, blk, HEAD_DIM), lambda bi, hi, qi: (bi, qi, hi))
    return pl.pallas_call(
        _sb_kernel,
        grid=(b, h, t // blk),
        in_specs=[q_spec, kv_spec, kv_spec],
        out_specs=q_spec,
        out_shape=jax.ShapeDtypeStruct((b, t, w), BF16),
        scratch_shapes=[pltpu.VMEM((blk, 1), F32), pltpu.VMEM((blk, HEAD_DIM), F32)],
        compiler_params=_cparams("parallel", "parallel", "arbitrary"),
        name="sb_prompt",
    )(q, k, v)


CONV_ROWS = 32
CONV_LANES = 512


def _layernorm_silu(y, g, b):
    mu = jnp.mean(y, axis=-1, keepdims=True)
    yc = y - mu
    var = jnp.mean(yc * yc, axis=-1, keepdims=True)
    yn = yc * lax.rsqrt(var + EPS) * g + b
    return yn * _sigmoid(yn)


def _conv_kernel(cur_ref, tail_ref, pre_ref, w_ref, bdw_ref, g_ref, b_ref, o_ref, win_ref, y_ref, *, tt):
    i = pl.program_id(1)
    d = cur_ref.shape[-1]

    @pl.when(i == 0)
    def _():
        win_ref[0:CONV_PAD, :] = pre_ref[...]

    @pl.when(i > 0)
    def _():
        win_ref[0:CONV_PAD, :] = tail_ref[...]

    win_ref[CONV_PAD:, :] = cur_ref[...]
    lead = CONV_PAD - (CONV_W - 1)
    for r in range(tt // CONV_ROWS):
        for c in range(d // CONV_LANES):
            ls = slice(c * CONV_LANES, (c + 1) * CONV_LANES)
            acc = jnp.zeros((CONV_ROWS, CONV_LANES), F32)
            for w in range(CONV_W):
                start = r * CONV_ROWS + lead + w
                acc = acc + w_ref[w:w + 1, ls] * win_ref[start:start + CONV_ROWS, ls]
            y_ref[r * CONV_ROWS:(r + 1) * CONV_ROWS, ls] = acc + bdw_ref[:, ls]
    o_ref[...] = _layernorm_silu(y_ref[...], g_ref[...], b_ref[...]).astype(o_ref.dtype)


def _conv_prompt(u, prefix, w_dw, b_dw, g_ln, b_ln, tt=256):
    b, t, d = u.shape
    assert t % tt == 0 and tt % CONV_PAD == 0
    per = tt // CONV_PAD
    w_pad = jnp.concatenate([w_dw, jnp.zeros((CONV_PAD - CONV_W, d), F32)], 0)
    vec = lambda x: x.reshape(1, d)
    vspec = pl.BlockSpec((1, d), lambda bi, i: (0, 0))
    return pl.pallas_call(
        functools.partial(_conv_kernel, tt=tt),
        grid=(b, t // tt),
        in_specs=[pl.BlockSpec((None, tt, d), lambda bi, i: (bi, i, 0)),
                  pl.BlockSpec((None, CONV_PAD, d), lambda bi, i: (bi, jnp.maximum(i * per - 1, 0), 0)),
                  pl.BlockSpec((None, CONV_PAD, d), lambda bi, i: (bi, 0, 0)),
                  pl.BlockSpec((CONV_PAD, d), lambda bi, i: (0, 0)),
                  vspec, vspec, vspec],
        out_specs=pl.BlockSpec((None, tt, d), lambda bi, i: (bi, i, 0)),
        out_shape=jax.ShapeDtypeStruct((b, t, d), BF16),
        scratch_shapes=[pltpu.VMEM((tt + CONV_PAD, d), F32), pltpu.VMEM((tt, d), F32)],
        compiler_params=_cparams("parallel", "arbitrary"),
        name="conv_prompt",
    )(u, u, prefix, w_pad, vec(b_dw), vec(g_ln), vec(b_ln))


def _conv_sample_kernel(st_ref, u_ref, w_ref, bdw_ref, g_ref, b_ref, o_ref):
    n = st_ref.shape[0]
    for s in range(n):
        st = st_ref[s]
        y = jnp.sum(st * w_ref[0:CONV_W - 1, :], axis=0, keepdims=True)
        y = y + u_ref[s:s + 1, :] * w_ref[CONV_W - 1:CONV_W, :] + bdw_ref[...]
        o_ref[s:s + 1, :] = _layernorm_silu(y, g_ref[...], b_ref[...]).astype(o_ref.dtype)


def _conv_sample(state, u, w_dw, b_dw, g_ln, b_ln):
    n, _, d = state.shape
    vec = lambda x: x.reshape(1, d)
    return pl.pallas_call(
        _conv_sample_kernel,
        out_shape=jax.ShapeDtypeStruct((n, d), F32),
        compiler_params=pltpu.CompilerParams(vmem_limit_bytes=V7X_VMEM_LIMIT_BYTES),
        name="conv_sample",
    )(state, u, w_dw, vec(b_dw), vec(g_ln), vec(b_ln))


def _xattn_kernel(q_ref, k_ref, v_ref, o_ref, *, tq):
    scale = X_HEAD_DIM ** -0.5
    for h in range(X_HEADS):
        sl = slice(h * X_HEAD_DIM, (h + 1) * X_HEAD_DIM)
        if tq == 1:
            q = q_ref[:, sl].astype(F32)
            s = jnp.sum(k_ref[:, sl] * q, axis=-1, keepdims=True) * scale
            p = jnp.exp(s - jnp.max(s, axis=0, keepdims=True))
            p = p / jnp.sum(p, axis=0, keepdims=True)
            o = jnp.sum(p * v_ref[:, sl], axis=0, keepdims=True)
        else:
            q = q_ref[:, sl].astype(BF16)
            s = lax.dot_general(q, k_ref[:, sl].astype(BF16), (((1,), (1,)), ((), ())),
                                preferred_element_type=F32) * scale
            p = jnp.exp(s - jnp.max(s, axis=1, keepdims=True))
            p = p / jnp.sum(p, axis=1, keepdims=True)
            o = jnp.dot(p.astype(BF16), v_ref[:, sl].astype(BF16), preferred_element_type=F32)
        o_ref[:, sl] = o.astype(o_ref.dtype)


def _xattn(q, mk, mv, layer, out_dtype, tq=512):
    n, t, xw = q.shape
    mem = mk.shape[2]
    tq = min(tq, t)
    kv_spec = pl.BlockSpec((None, None, mem, xw), lambda ni, i: (layer, ni, 0, 0))
    q_spec = pl.BlockSpec((None, tq, xw), lambda ni, i: (ni, i, 0))
    return pl.pallas_call(
        functools.partial(_xattn_kernel, tq=tq),
        grid=(n, t // tq),
        in_specs=[q_spec, kv_spec, kv_spec],
        out_specs=q_spec,
        out_shape=jax.ShapeDtypeStruct((n, t, xw), out_dtype),
        compiler_params=_cparams("parallel", "parallel"),
        name="xattn",
    )(q, mk, mv)


def _moba_select_kernel(pt_ref, k0_ref, k1_ref, q_ref, sel_ref, km_ref, *, nblk):
    del pt_ref
    b = pl.program_id(1)
    km_ref[b] = (jnp.sum(k0_ref[...], axis=0) + jnp.sum(k1_ref[...], axis=0)) * (1.0 / MOBA_BLOCK)

    @pl.when(b == nblk - 1)
    def _():
        gate = jnp.sum(km_ref[...] * q_ref[...][None], axis=-1, keepdims=True)
        bidx = lax.broadcasted_iota(jnp.int32, gate.shape, 0)
        rank = jnp.zeros(gate.shape, jnp.int32)
        for b2 in range(nblk):
            gb = gate[b2:b2 + 1]
            rank = rank + ((gb > gate) | ((gb == gate) & (b2 < bidx))).astype(jnp.int32)
        for s in range(MOBA_TOPK):
            sel_ref[s] = jnp.sum(jnp.where(rank == s, bidx, 0), axis=0)


def _moba_sample_select(cache_k, layer, page_table, q):
    n, h, dh = q.shape
    n_pages = page_table.shape[1]
    pages_per_blk = MOBA_BLOCK // PAGE_SIZE
    assert pages_per_blk == 2
    nblk = n_pages // pages_per_blk
    assert nblk >= MOBA_TOPK
    page_spec = lambda which: pl.BlockSpec(
        (None, None, PAGE_SIZE, h, dh),
        lambda ni, b, pt: (layer, pt[ni, pages_per_blk * b + which], 0, 0, 0))
    sel = pl.pallas_call(
        functools.partial(_moba_select_kernel, nblk=nblk),
        grid_spec=pltpu.PrefetchScalarGridSpec(
            num_scalar_prefetch=1,
            grid=(n, nblk),
            in_specs=[page_spec(0), page_spec(1),
                      pl.BlockSpec((None, h, dh), lambda ni, b, pt: (ni, 0, 0))],
            out_specs=pl.BlockSpec((None, MOBA_TOPK, h, 1), lambda ni, b, pt: (ni, 0, 0, 0)),
            scratch_shapes=[pltpu.VMEM((nblk, h, dh), F32)]),
        out_shape=jax.ShapeDtypeStruct((n, MOBA_TOPK, h, 1), jnp.int32),
        compiler_params=_cparams("parallel", "arbitrary"),
        name="moba_sample_select",
    )(page_table, cache_k, cache_k, q)
    return sel.reshape(n, MOBA_TOPK, h)


def _moba_attend_kernel(pt_ref, sel_ref, k_ref, v_ref, q_ref, kn_ref, vn_ref, o_ref,
                        m_ref, l_ref, acc_ref, *, n_steps):
    del pt_ref, sel_ref
    t = pl.program_id(2)
    scale = HEAD_DIM ** -0.5
    q = q_ref[...]

    @pl.when(t == 0)
    def _():
        m_ref[...] = jnp.sum(q * kn_ref[...], axis=-1, keepdims=True) * scale
        l_ref[...] = jnp.ones((1, 1), F32)
        acc_ref[...] = vn_ref[...]

    s = jnp.sum(k_ref[...] * q, axis=-1, keepdims=True) * scale
    m_prev = m_ref[...]
    m_new = jnp.maximum(m_prev, jnp.max(s, axis=0, keepdims=True))
    alpha = jnp.exp(m_prev - m_new)
    p = jnp.exp(s - m_new)
    l_ref[...] = alpha * l_ref[...] + jnp.sum(p, axis=0, keepdims=True)
    acc_ref[...] = alpha * acc_ref[...] + jnp.sum(p * v_ref[...], axis=0, keepdims=True)
    m_ref[...] = m_new

    @pl.when(t == n_steps - 1)
    def _():
        o_ref[...] = acc_ref[...] / l_ref[...]


def _moba_sample_attend(cache_k, cache_v, layer, page_table, sel, q, k_new, v_new):
    n, h, dh = q.shape
    pages_per_blk = MOBA_BLOCK // PAGE_SIZE
    n_steps = MOBA_TOPK * pages_per_blk
    page_spec = pl.BlockSpec(
        (None, None, PAGE_SIZE, dh),
        lambda ni, hi, t, pt, sl: (layer, pt[ni, sl[ni, (t // pages_per_blk) * h + hi] * pages_per_blk
                                             + t % pages_per_blk], 0, hi))
    flat = lambda c: c.reshape(c.shape[0], c.shape[1], PAGE_SIZE, h * dh)
    vec_spec = pl.BlockSpec((None, None, 1, dh), lambda ni, hi, t, pt, sl: (ni, hi, 0, 0))
    r4 = lambda x: x.reshape(n, h, 1, dh)
    out = pl.pallas_call(
        functools.partial(_moba_attend_kernel, n_steps=n_steps),
        grid_spec=pltpu.PrefetchScalarGridSpec(
            num_scalar_prefetch=2,
            grid=(n, h, n_steps),
            in_specs=[page_spec, page_spec, vec_spec, vec_spec, vec_spec],
            out_specs=vec_spec,
            scratch_shapes=[pltpu.VMEM((1, 1), F32), pltpu.VMEM((1, 1), F32),
                            pltpu.VMEM((1, dh), F32)]),
        out_shape=jax.ShapeDtypeStruct((n, h, 1, dh), F32),
        compiler_params=_cparams("parallel", "parallel", "arbitrary"),
        name="moba_sample_attend",
    )(page_table, sel.reshape(n, MOBA_TOPK * h), flat(cache_k), flat(cache_v), r4(q), r4(k_new), r4(v_new))
    return out.reshape(n, h, dh)


def _sb_sample_kernel(pt_ref, k_ref, v_ref, q_ref, o_ref, c_ref, acc_ref, *, n_pages):
    del pt_ref
    j = pl.program_id(1)
    scale = HEAD_DIM ** -0.5

    @pl.when(j == 0)
    def _():
        c_ref[...] = jnp.zeros(c_ref.shape, F32)
        acc_ref[...] = jnp.zeros(acc_ref.shape, F32)

    z = jnp.sum(k_ref[...] * q_ref[...][None], axis=-1, keepdims=True) * scale
    ls_pos, lk = _log_sigmoid_pair(z)
    incl = lk
    shift = 1
    while shift < PAGE_SIZE:
        incl = incl + jnp.concatenate(
            [incl[shift:], jnp.zeros((shift,) + incl.shape[1:], F32)], axis=0)
        shift *= 2
    a = jnp.exp(ls_pos + (incl - lk) + c_ref[...][None])
    acc_ref[...] += jnp.sum(a * v_ref[...], axis=0)
    c_ref[...] += incl[0]

    @pl.when(j == n_pages - 1)
    def _():
        o_ref[...] = acc_ref[...]


def _sb_sample(cache_k, cache_v, layer, page_table, q):
    n, h, dh = q.shape
    n_pages = page_table.shape[1]
    page_spec = pl.BlockSpec((None, None, PAGE_SIZE, h, dh),
                             lambda ni, j, pt: (layer, pt[ni, n_pages - 1 - j], 0, 0, 0))
    vec_spec = pl.BlockSpec((None, h, dh), lambda ni, j, pt: (ni, 0, 0))
    return pl.pallas_call(
        functools.partial(_sb_sample_kernel, n_pages=n_pages),
        grid_spec=pltpu.PrefetchScalarGridSpec(
            num_scalar_prefetch=1,
            grid=(n, n_pages),
            in_specs=[page_spec, page_spec, vec_spec],
            out_specs=vec_spec,
            scratch_shapes=[pltpu.VMEM((h, 1), F32), pltpu.VMEM((h, dh), F32)]),
        out_shape=jax.ShapeDtypeStruct((n, h, dh), F32),
        compiler_params=_cparams("parallel", "arbitrary"),
        name="sb_sample",
    )(page_table, cache_k, cache_v, q)


def _trunk(x3, pos, w, *, mem=None, mem_k=None, mem_v=None, paged=None, conv_state=None):
    n, t, d = x3.shape
    m = n * t
    depth = w["g_mix"].shape[0]
    hd = w["w_out"].shape[1] // 2
    xw = w["w_xq"].shape[2]
    d_ff = w["w_ffn_gate"].shape[2]
    act = BF16 if t > 1 else F32
    tables = _rope_tables(pos)
    if t == 1:
        tables = tuple(jnp.broadcast_to(tb, (n, HEAD_DIM)) for tb in tables)
    x = x3.reshape(m, d)
    nka, nva, nkb, nvb, nmk, nmv, nconv = [], [], [], [], [], [], []
    for l in range(depth):
        h = _rmsnorm(x, w["g_mix"][l], act)
        if l % 2 == 0:
            i = l // 2
            w_in = w["w_in"][i]
            proj = lambda j, dt: _mm(h, w_in, n=hd, col_off=j * hd, out_dtype=dt, tn=1024)
            qa, ka, va = proj(0, F32), proj(1, F32), proj(2, F32)
            qb, kb, vb = proj(3, act), proj(4, F32), proj(5, F32)
            rot_len = t if t > 1 else n
            qa = _rotary(qa, tables, rot_len)
            ka = _rotary(ka, tables, rot_len)
            nka.append(ka); nva.append(va); nkb.append(kb); nvb.append(vb)
            if paged is None:
                r3 = lambda a: a.reshape(n, t, hd)
                oa = _moba_prompt(r3(qa), r3(ka), r3(va)).reshape(m, hd)
                ob = _sb_prompt(r3(qb), r3(kb), r3(vb)).reshape(m, hd)
            else:
                ck_a, cv_a, ck_b, cv_b, table = paged
                heads = hd // HEAD_DIM
                rh = lambda a: a.reshape(n, heads, HEAD_DIM)
                sel = _moba_sample_select(ck_a, i, table, rh(qa))
                oa = _moba_sample_attend(ck_a, cv_a, i, table, sel, rh(qa), rh(ka), rh(va)).reshape(m, hd)
                ob = _sb_sample(ck_b, cv_b, i, table, rh(qb)).reshape(m, hd)
            w_out = w["w_out"][i]
            x = _mm(oa, w_out, n=d, k=hd, row_off=0, residual=x, tk=hd)
            x = _mm(ob, w_out, n=d, k=hd, row_off=hd, residual=x, tk=hd)
        else:
            j = l // 2
            w_pw1 = w["w_pw1"][j]
            u = _mm(h, w_pw1, w2=w_pw1, n=d, col_off=0, col_off2=d, epilogue="glu")
            if conv_state is None:
                prefix = jnp.zeros((n, CONV_PAD, d), F32)
                y = _conv_prompt(u.reshape(n, t, d), prefix, w["w_dw"][j], w["b_dw"][j],
                                 w["g_conv_ln"][j], w["b_conv_ln"][j]).reshape(m, d)
                nconv.append(u.reshape(n, t, d)[:, t - (CONV_W - 1):])
            else:
                st = conv_state[j]
                y = _conv_sample(st, u, w["w_dw"][j], w["b_dw"][j], w["g_conv_ln"][j], w["b_conv_ln"][j])
                nconv.append(jnp.concatenate([st, u[:, None, :]], 1)[:, 1:])
            x = _mm(y, w["w_pw2"][j], n=d, residual=x)
        if mem is not None:
            mn = _rmsnorm(mem.reshape(-1, d), w["g_mem"][l], BF16)
            mk = _mm(mn, w["w_xkv"][l], n=xw, col_off=0)
            mv = _mm(mn, w["w_xkv"][l], n=xw, col_off=xw)
            mem_len = mem.shape[1]
            nmk.append(mk.reshape(n, mem_len, X_HEADS, X_HEAD_DIM))
            nmv.append(mv.reshape(n, mem_len, X_HEADS, X_HEAD_DIM))
            mk4, mv4, lyr = mk.reshape(1, n, mem_len, xw), mv.reshape(1, n, mem_len, xw), 0
        else:
            mk4 = mem_k.reshape(mem_k.shape[0], n, mem_k.shape[2], xw)
            mv4 = mem_v.reshape(mem_v.shape[0], n, mem_v.shape[2], xw)
            lyr = l
        hx = _rmsnorm(x, w["g_xattn"][l], act)
        q = _mm(hx, w["w_xq"][l], n=xw, out_dtype=act)
        o = _xattn(q.reshape(n, t, xw), mk4, mv4, lyr, act).reshape(m, xw)
        x = _mm(o, w["w_xo"][l], n=d, residual=x)
        hf = _rmsnorm(x, w["g_ffn"][l], act)
        g = _mm(hf, w["w_ffn_gate"][l], w2=w["w_ffn_up"][l], n=d_ff, epilogue="swiglu",
                out_dtype=act, tn=256)
        x = _mm(g, w["w_ffn_down"][l], n=d, residual=x, tk=d_ff // 2)
    y = _rmsnorm(x, w["g_final"], F32).reshape(n, t, d)
    return y, nka, nva, nkb, nvb, nmk, nmv, nconv


def kernel(x_prompt, x_sample, cache_k_a, cache_v_a, cache_k_b, cache_v_b, cache_mem_k, cache_mem_v,
           state_conv, page_table, mem_prompt, g_mix, w_in, w_out, w_pw1, w_dw, b_dw, g_conv_ln,
           b_conv_ln, w_pw2, g_mem, g_xattn, w_xq, w_xkv, w_xo, g_ffn, w_ffn_gate, w_ffn_up,
           w_ffn_down, g_final):
    cast = lambda a: a.astype(BF16)
    w = {"g_mix": g_mix, "w_in": cast(w_in), "w_out": cast(w_out), "w_pw1": cast(w_pw1), "w_dw": w_dw,
         "b_dw": b_dw, "g_conv_ln": g_conv_ln, "b_conv_ln": b_conv_ln, "w_pw2": cast(w_pw2),
         "g_mem": g_mem, "g_xattn": g_xattn, "w_xq": cast(w_xq), "w_xkv": cast(w_xkv),
         "w_xo": cast(w_xo), "g_ffn": g_ffn, "w_ffn_gate": cast(w_ffn_gate),
         "w_ffn_up": cast(w_ffn_up), "w_ffn_down": cast(w_ffn_down), "g_final": g_final}
    nb, t, _ = x_prompt.shape
    ns, ts, _ = x_sample.shape
    past_len = page_table.shape[1] * PAGE_SIZE
    pos_p = jnp.arange(t, dtype=jnp.int32)
    pos_s = past_len + jnp.arange(ts, dtype=jnp.int32)
    heads = w_out.shape[1] // 2 // HEAD_DIM

    y_p, ka_p, va_p, kb_p, vb_p, mk_p, mv_p, cv_p = _trunk(x_prompt, pos_p, w, mem=mem_prompt)
    y_s, ka_s, va_s, kb_s, vb_s, _, _, cv_s = _trunk(
        x_sample, pos_s, w, mem_k=cache_mem_k, mem_v=cache_mem_v,
        paged=(cache_k_a, cache_v_a, cache_k_b, cache_v_b, page_table), conv_state=state_conv)

    def heads_p(xs):
        return jnp.stack([a.reshape(nb, t, heads, HEAD_DIM) for a in xs])

    def heads_s(xs):
        return jnp.stack([a.reshape(ns, ts, heads, HEAD_DIM) for a in xs])

    return (y_p, y_s,
            heads_p(ka_p), heads_p(va_p), heads_p(kb_p), heads_p(vb_p),
            jnp.stack(mk_p), jnp.stack(mv_p), jnp.stack(cv_p),
            heads_s(ka_s), heads_s(va_s), heads_s(kb_s), heads_s(vb_s), jnp.stack(cv_s))
```

```python
import functools

import jax
import jax.numpy as jnp
from jax import lax
from jax.experimental import pallas as pl
from jax.experimental.pallas import tpu as pltpu

F32 = jnp.float32
BF16 = jnp.bfloat16

HEAD_DIM = 128
ROT_DIM = HEAD_DIM // 4
ROPE_THETA = 500000.0
MOBA_BLOCK = 256
MOBA_TOPK = 3
CONV_W = 31
X_HEADS = 4
X_HEAD_DIM = 128
EPS = 1e-6
PAGE_SIZE = 128

V7X_VMEM_LIMIT_BYTES = 56 * 1024 * 1024
SUBLANES = 8
NEG_BIG = -1e30
CONV_PAD = 32


def _cparams(*sem):
    return pltpu.CompilerParams(dimension_semantics=sem, vmem_limit_bytes=V7X_VMEM_LIMIT_BYTES)


def _sigmoid(x):
    return 1.0 / (1.0 + jnp.exp(-x))


def _rmsnorm_kernel(x_ref, g_ref, o_ref):
    x = x_ref[...]
    ms = jnp.mean(x * x, axis=-1, keepdims=True)
    o_ref[...] = (x * lax.rsqrt(ms + EPS) * g_ref[...]).astype(o_ref.dtype)


def _rmsnorm(x, g, out_dtype, tm=256):
    m, d = x.shape
    tm = min(tm, m)
    return pl.pallas_call(
        _rmsnorm_kernel,
        grid=(m // tm,),
        in_specs=[pl.BlockSpec((tm, d), lambda i: (i, 0)),
                  pl.BlockSpec((1, d), lambda i: (0, 0))],
        out_specs=pl.BlockSpec((tm, d), lambda i: (i, 0)),
        out_shape=jax.ShapeDtypeStruct((m, d), out_dtype),
        compiler_params=_cparams("parallel"),
        name="rmsnorm",
    )(x, g.reshape(1, d))


def _mm_kernel(*refs, nk, n_w, has_res, epilogue):
    a_ref = refs[0]
    w_refs = refs[1:1 + n_w]
    pos = 1 + n_w
    r_ref = refs[pos] if has_res else None
    pos += int(has_res)
    o_ref = refs[pos]
    acc_refs = refs[pos + 1:]

    a = a_ref[...].astype(BF16)
    ps = [jnp.dot(a, w_ref[...].astype(BF16), preferred_element_type=F32) for w_ref in w_refs]

    def finish(vals):
        if epilogue == "swiglu":
            y = vals[0] * _sigmoid(vals[0]) * vals[1]
        elif epilogue == "glu":
            y = vals[0] * _sigmoid(vals[1])
        else:
            y = vals[0]
        if has_res:
            y = r_ref[...] + y
        o_ref[...] = y.astype(o_ref.dtype)

    if nk == 1:
        finish(ps)
    else:
        kk = pl.program_id(2)

        @pl.when(kk == 0)
        def _():
            for acc, p in zip(acc_refs, ps):
                acc[...] = p

        @pl.when(kk > 0)
        def _():
            for acc, p in zip(acc_refs, ps):
                acc[...] += p

        @pl.when(kk == nk - 1)
        def _():
            finish([acc[...] for acc in acc_refs])


def _mm(a, w, layer, *, n, k=None, col_off=0, row_off=0, w2=None, col_off2=0, epilogue="none",
        residual=None, out_dtype=F32, tm=1024, tn=512, tk=None):
    m, ka = a.shape
    k = ka if k is None else k
    tm = min(tm, m)
    tn = min(tn, n)
    tk = k if tk is None else tk
    nk = k // tk
    assert m % tm == 0 and n % tn == 0 and k % tk == 0
    assert col_off % tn == 0 and col_off2 % tn == 0 and row_off % tk == 0
    cb, cb2, rb = col_off // tn, col_off2 // tn, row_off // tk
    ws = [w] if w2 is None else [w, w2]
    cbs = [cb, cb2]
    in_specs = [pl.BlockSpec((tm, tk), lambda i, j, kk: (i, kk))]
    for idx in range(len(ws)):
        in_specs.append(pl.BlockSpec((None, tk, tn), functools.partial(
            lambda i, j, kk, c: (layer, rb + kk, c + j), c=cbs[idx])))
    args = [a] + ws
    if residual is not None:
        in_specs.append(pl.BlockSpec((tm, tn), lambda i, j, kk: (i, j)))
        args.append(residual)
    scratch = [pltpu.VMEM((tm, tn), F32) for _ in ws] if nk > 1 else []
    return pl.pallas_call(
        functools.partial(_mm_kernel, nk=nk, n_w=len(ws), has_res=residual is not None,
                          epilogue=epilogue),
        grid=(m // tm, n // tn, nk),
        in_specs=in_specs,
        out_specs=pl.BlockSpec((tm, tn), lambda i, j, kk: (i, j)),
        out_shape=jax.ShapeDtypeStruct((m, n), out_dtype),
        scratch_shapes=scratch,
        compiler_params=_cparams("parallel", "parallel", "arbitrary"),
        name="mm_" + epilogue,
    )(*args)


def _rope_tables(pos):
    half = ROT_DIM // 2
    inv = ROPE_THETA ** (-jnp.arange(0, ROT_DIM, 2, dtype=F32) / ROT_DIM)
    ang = pos.astype(F32)[:, None] * inv[None, :]
    cos, sin = jnp.cos(ang), jnp.sin(ang)
    t = pos.shape[0]
    rest = HEAD_DIM - ROT_DIM
    c = jnp.concatenate([cos, cos, jnp.ones((t, rest), F32)], 1)
    s1 = jnp.concatenate([-sin, jnp.zeros((t, half + rest), F32)], 1)
    s2 = jnp.concatenate([jnp.zeros((t, half), F32), sin, jnp.zeros((t, rest), F32)], 1)
    return c, s1, s2


def _rotary_kernel(x_ref, c_ref, s1_ref, s2_ref, o_ref, *, n_heads):
    half = ROT_DIM // 2
    c, s1, s2 = c_ref[...], s1_ref[...], s2_ref[...]
    for h in range(n_heads):
        sl = slice(h * HEAD_DIM, (h + 1) * HEAD_DIM)
        x = x_ref[:, sl]
        y = x * c + pltpu.roll(x, HEAD_DIM - half, 1) * s1 + pltpu.roll(x, half, 1) * s2
        o_ref[:, sl] = y.astype(o_ref.dtype)


def _rotary(x, tables, t_len, tm=256):
    m, w = x.shape
    tm = min(tm, t_len, m)
    nt = t_len // tm
    tab_spec = pl.BlockSpec((tm, HEAD_DIM), lambda i: (i % nt, 0))
    return pl.pallas_call(
        functools.partial(_rotary_kernel, n_heads=w // HEAD_DIM),
        grid=(m // tm,),
        in_specs=[pl.BlockSpec((tm, w), lambda i: (i, 0)), tab_spec, tab_spec, tab_spec],
        out_specs=pl.BlockSpec((tm, w), lambda i: (i, 0)),
        out_shape=jax.ShapeDtypeStruct((m, w), F32),
        compiler_params=_cparams("parallel"),
        name="rotary",
    )(x, *tables)


def _moba_tile(c, q_ref, o_ref, k16_ref, vt_ref, kmean_ref, pen_ref, *, nb):
    blk = MOBA_BLOCK
    scale = HEAD_DIM ** -0.5
    q_t = q_ref[...].T
    gate = lax.dot_general(kmean_ref[...], q_t, (((1,), (0,)), ((), ())),
                           precision=lax.Precision.HIGHEST, preferred_element_type=F32)
    bidx = lax.broadcasted_iota(jnp.int32, (nb, blk), 0)
    valid = bidx < c
    gm = jnp.where(valid, gate, -jnp.inf)
    rank = jnp.zeros((nb, blk), jnp.int32)
    for b2 in range(c):
        gb = gm[b2:b2 + 1, :]
        rank = rank + ((gb > gm) | ((gb == gm) & (b2 < bidx))).astype(jnp.int32)
    pen_ref[...] = jnp.where(valid & (rank < MOBA_TOPK), 0.0, NEG_BIG)

    q16_t = q_t.astype(BF16)
    krow = lax.broadcasted_iota(jnp.int32, (blk, blk), 0)
    qcol = lax.broadcasted_iota(jnp.int32, (blk, blk), 1)
    s_blocks = []
    for b in range(c + 1):
        s = jnp.dot(k16_ref[b * blk:(b + 1) * blk, :], q16_t, preferred_element_type=F32) * scale
        if b == c:
            s_blocks.append(jnp.where(krow <= qcol, s, NEG_BIG))
        else:
            s_blocks.append(s + pen_ref[b:b + 1, :])
    m = s_blocks[0]
    for s in s_blocks[1:]:
        m = jnp.maximum(m, s)
    m = jnp.max(m, axis=0, keepdims=True)
    p_blocks = [jnp.exp(s - m) for s in s_blocks]
    psum = p_blocks[0]
    for p in p_blocks[1:]:
        psum = psum + p
    l = jnp.sum(psum, axis=0, keepdims=True)
    p_all = jnp.concatenate([p.astype(BF16) for p in p_blocks], axis=0)
    o_t = jnp.dot(vt_ref[:, 0:(c + 1) * blk], p_all, preferred_element_type=F32)
    o_ref[...] = (o_t / l).T.astype(o_ref.dtype)


def _moba_kernel(q_ref, k_ref, v_ref, o_ref, k16_ref, vt_ref, kmean_ref, pen0_ref, pen1_ref, *, nb):
    blk = MOBA_BLOCK
    k = k_ref[...]
    kmean_ref[...] = jnp.mean(k.reshape(nb, blk, HEAD_DIM), axis=1)
    k16_ref[...] = k.astype(BF16)
    vt_ref[...] = v_ref[...].T.astype(BF16)
    for c in range(nb):
        rows = slice(c * blk, (c + 1) * blk)
        _moba_tile(c, q_ref.at[rows, :], o_ref.at[rows, :], k16_ref, vt_ref, kmean_ref,
                   pen1_ref if c % 2 else pen0_ref, nb=nb)


def _moba_prompt(q, k, v):
    b, t, w = q.shape
    h = w // HEAD_DIM
    blk = MOBA_BLOCK
    nb = max(-(-t // blk), MOBA_TOPK + 1)
    assert t % blk == 0 and nb * blk == t
    spec = pl.BlockSpec((None, t, HEAD_DIM), lambda bi, hi: (bi, 0, hi))
    return pl.pallas_call(
        functools.partial(_moba_kernel, nb=nb),
        grid=(b, h),
        in_specs=[spec, spec, spec],
        out_specs=spec,
        out_shape=jax.ShapeDtypeStruct((b, t, w), BF16),
        scratch_shapes=[pltpu.VMEM((t, HEAD_DIM), BF16), pltpu.VMEM((HEAD_DIM, t), BF16),
                        pltpu.VMEM((nb, HEAD_DIM), F32), pltpu.VMEM((nb, blk), F32),
                        pltpu.VMEM((nb, blk), F32)],
        compiler_params=_cparams("parallel", "parallel"),
        name="moba_prompt",
    )(q, k, v)


SB_BLOCK = 256


def _log_sigmoid_pair(z):
    soft = jnp.log(1.0 + jnp.exp(-jnp.abs(z)))
    ls_pos = jnp.minimum(z, 0.0) - soft
    return ls_pos, ls_pos - z


def _sb_tile(c, q_ref, o_ref, k16_ref, vt_ref, z_ref, pre_ref, hi_ref, lo_ref, a_ref, tot_ref):
    blk = SB_BLOCK
    scale = HEAD_DIM ** -0.5
    n_keys = (c + 1) * blk
    q16_t = q_ref[...].astype(F32).T.astype(BF16)
    krow = lax.broadcasted_iota(jnp.int32, (blk, blk), 0)
    qcol = lax.broadcasted_iota(jnp.int32, (blk, blk), 1)
    strictly_past = krow < qcol
    later = (qcol > krow).astype(BF16)
    blocks = [slice(kb * blk, (kb + 1) * blk) for kb in range(c + 1)]

    z_ref[0:n_keys, :] = jnp.dot(k16_ref[0:n_keys, :], q16_t, preferred_element_type=F32)
    for kb, ks in enumerate(blocks):
        ls_pos, lk = _log_sigmoid_pair(z_ref[ks, :] * scale)
        if kb == c:
            lk = jnp.where(strictly_past, lk, 0.0)
        lk_hi = lk.astype(BF16)
        hi_ref[ks, :] = lk_hi
        lo_ref[ks, :] = (lk - lk_hi.astype(F32)).astype(BF16)
        pre_ref[ks, :] = ls_pos
        tot_ref[kb:kb + 1, :] = lk[0:1, :]
    for kb, ks in enumerate(blocks):
        after = (jnp.dot(later, hi_ref[ks, :], preferred_element_type=F32)
                 + jnp.dot(later, lo_ref[ks, :], preferred_element_type=F32))
        pre_ref[ks, :] += after
        tot_ref[kb:kb + 1, :] += after[0:1, :]
    carry = jnp.zeros((1, blk), F32)
    for kb in range(c, -1, -1):
        ks = blocks[kb]
        a = jnp.exp(pre_ref[ks, :] + carry)
        if kb == c:
            a = jnp.where(strictly_past, a, 0.0)
        a_ref[ks, :] = a.astype(BF16)
        carry = carry + tot_ref[kb:kb + 1, :]
    o_t = jnp.dot(vt_ref[:, 0:n_keys], a_ref[0:n_keys, :], preferred_element_type=F32)
    o_ref[...] = o_t.T.astype(o_ref.dtype)


def _sb_kernel(q_ref, k_ref, v_ref, o_ref, k16_ref, vt_ref, *stage_refs, nq):
    blk = SB_BLOCK
    k16_ref[...] = k_ref[...].astype(BF16)
    vt_ref[...] = v_ref[...].T.astype(BF16)
    half = len(stage_refs) // 2
    for c in range(nq):
        rows = slice(c * blk, (c + 1) * blk)
        _sb_tile(c, q_ref.at[rows, :], o_ref.at[rows, :], k16_ref, vt_ref,
                 *stage_refs[(c % 2) * half:(c % 2 + 1) * half])


def _sb_prompt(q, k, v):
    b, t, w = q.shape
    h = w // HEAD_DIM
    blk = SB_BLOCK
    assert t % blk == 0
    spec = pl.BlockSpec((None, t, HEAD_DIM), lambda bi, hi: (bi, 0, hi))
    stage = [pltpu.VMEM((t, blk), F32), pltpu.VMEM((t, blk), F32),
             pltpu.VMEM((t, blk), BF16), pltpu.VMEM((t, blk), BF16),
             pltpu.VMEM((t, blk), BF16), pltpu.VMEM((t // blk, blk), F32)]
    return pl.pallas_call(
        functools.partial(_sb_kernel, nq=t // blk),
        grid=(b, h),
        in_specs=[spec, spec, spec],
        out_specs=spec,
        out_shape=jax.ShapeDtypeStruct((b, t, w), BF16),
        scratch_shapes=[pltpu.VMEM((t, HEAD_DIM), BF16), pltpu.VMEM((HEAD_DIM, t), BF16)] + stage + stage,
        compiler_params=_cparams("parallel", "parallel"),
        name="sb_prompt",
    )(q, k, v)


CONV_LANES = 256
CONV_LEAD = CONV_PAD - (CONV_W - 1)


def _layernorm_silu(y, g, b):
    mu = jnp.mean(y, axis=-1, keepdims=True)
    yc = y - mu
    var = jnp.mean(yc * yc, axis=-1, keepdims=True)
    yn = yc * lax.rsqrt(var + EPS) * g + b
    return yn * _sigmoid(yn)


def _conv_tap_groups():
    return [[CONV_LEAD + w for w in range(CONV_W) if (CONV_LEAD + w) % SUBLANES == res]
            for res in range(SUBLANES)]


def _conv_kernel(cur_ref, tail_ref, pre_ref, w_ref, bdw_ref, g_ref, b_ref, o_ref, win_ref, y_ref, sh_ref, *, tt):
    i = pl.program_id(1)
    d = cur_ref.shape[-1]

    @pl.when(i == 0)
    def _():
        win_ref[0:CONV_PAD, :] = pre_ref[...]

    @pl.when(i > 0)
    def _():
        win_ref[0:CONV_PAD, :] = tail_ref[...]

    win_ref[CONV_PAD:, :] = cur_ref[...]
    groups = _conv_tap_groups()
    for c in range(d // CONV_LANES):
        ls = slice(c * CONV_LANES, (c + 1) * CONV_LANES)
        buf = c % 2
        for res, offs in enumerate(groups):
            n = offs[-1] - offs[0] + tt
            sh_ref[buf, res, 0:n, :] = win_ref[offs[0]:offs[0] + n, ls]
        acc = jnp.zeros((tt, CONV_LANES), F32)
        for res, offs in enumerate(groups):
            for off in offs:
                w = off - CONV_LEAD
                acc = acc + w_ref[w:w + 1, ls] * sh_ref[buf, res, off - offs[0]:off - offs[0] + tt, :]
        y_ref[:, ls] = acc + bdw_ref[:, ls]
    o_ref[...] = _layernorm_silu(y_ref[...], g_ref[...], b_ref[...]).astype(o_ref.dtype)


def _conv_prompt(u, prefix, w_dw, b_dw, g_ln, b_ln, tt=128):
    b, t, d = u.shape
    assert t % tt == 0 and tt % CONV_PAD == 0 and d % CONV_LANES == 0
    per = tt // CONV_PAD
    w_pad = jnp.concatenate([w_dw, jnp.zeros((CONV_PAD - CONV_W, d), F32)], 0)
    vec = lambda x: x.reshape(1, d)
    vspec = pl.BlockSpec((1, d), lambda bi, i: (0, 0))
    shifted_rows = tt + CONV_PAD - SUBLANES
    return pl.pallas_call(
        functools.partial(_conv_kernel, tt=tt),
        grid=(b, t // tt),
        in_specs=[pl.BlockSpec((None, tt, d), lambda bi, i: (bi, i, 0)),
                  pl.BlockSpec((None, CONV_PAD, d), lambda bi, i: (bi, jnp.maximum(i * per - 1, 0), 0)),
                  pl.BlockSpec((None, CONV_PAD, d), lambda bi, i: (bi, 0, 0)),
                  pl.BlockSpec((CONV_PAD, d), lambda bi, i: (0, 0)),
                  vspec, vspec, vspec],
        out_specs=pl.BlockSpec((None, tt, d), lambda bi, i: (bi, i, 0)),
        out_shape=jax.ShapeDtypeStruct((b, t, d), BF16),
        scratch_shapes=[pltpu.VMEM((tt + CONV_PAD, d), F32), pltpu.VMEM((tt, d), F32),
                        pltpu.VMEM((2, SUBLANES, shifted_rows, CONV_LANES), F32)],
        compiler_params=_cparams("parallel", "arbitrary"),
        name="conv_prompt",
    )(u, u, prefix, w_pad, vec(b_dw), vec(g_ln), vec(b_ln))


def _conv_sample_kernel(st_ref, u_ref, w_ref, bdw_ref, g_ref, b_ref, o_ref):
    n = st_ref.shape[0]
    for s in range(n):
        st = st_ref[s]
        y = jnp.sum(st * w_ref[0:CONV_W - 1, :], axis=0, keepdims=True)
        y = y + u_ref[s:s + 1, :] * w_ref[CONV_W - 1:CONV_W, :] + bdw_ref[...]
        o_ref[s:s + 1, :] = _layernorm_silu(y, g_ref[...], b_ref[...]).astype(o_ref.dtype)


def _conv_sample(state, u, w_dw, b_dw, g_ln, b_ln):
    n, _, d = state.shape
    vec = lambda x: x.reshape(1, d)
    return pl.pallas_call(
        _conv_sample_kernel,
        out_shape=jax.ShapeDtypeStruct((n, d), F32),
        compiler_params=pltpu.CompilerParams(vmem_limit_bytes=V7X_VMEM_LIMIT_BYTES),
        name="conv_sample",
    )(state, u, w_dw, vec(b_dw), vec(g_ln), vec(b_ln))


def _xattn_kernel(q_ref, k_ref, v_ref, o_ref, *, tq):
    scale = X_HEAD_DIM ** -0.5
    for h in range(X_HEADS):
        sl = slice(h * X_HEAD_DIM, (h + 1) * X_HEAD_DIM)
        if tq == 1:
            q = q_ref[:, sl].astype(F32)
            s = jnp.sum(k_ref[:, sl] * q, axis=-1, keepdims=True) * scale
            p = jnp.exp(s - jnp.max(s, axis=0, keepdims=True))
            p = p / jnp.sum(p, axis=0, keepdims=True)
            o = jnp.sum(p * v_ref[:, sl], axis=0, keepdims=True)
        else:
            q = q_ref[:, sl].astype(BF16)
            s = lax.dot_general(q, k_ref[:, sl].astype(BF16), (((1,), (1,)), ((), ())),
                                preferred_element_type=F32) * scale
            p = jnp.exp(s - jnp.max(s, axis=1, keepdims=True))
            p = p / jnp.sum(p, axis=1, keepdims=True)
            o = jnp.dot(p.astype(BF16), v_ref[:, sl].astype(BF16), preferred_element_type=F32)
        o_ref[:, sl] = o.astype(o_ref.dtype)


def _xattn(q, mk, mv, layer, out_dtype, tq=512):
    n, t, xw = q.shape
    mem = mk.shape[2]
    tq = min(tq, t)
    kv_spec = pl.BlockSpec((None, None, mem, xw), lambda ni, i: (layer, ni, 0, 0))
    q_spec = pl.BlockSpec((None, tq, xw), lambda ni, i: (ni, i, 0))
    return pl.pallas_call(
        functools.partial(_xattn_kernel, tq=tq),
        grid=(n, t // tq),
        in_specs=[q_spec, kv_spec, kv_spec],
        out_specs=q_spec,
        out_shape=jax.ShapeDtypeStruct((n, t, xw), out_dtype),
        compiler_params=_cparams("parallel", "parallel"),
        name="xattn",
    )(q, mk, mv)


def _moba_select_kernel(pt_ref, k0_ref, k1_ref, q_ref, sel_ref, km_ref, *, nblk):
    del pt_ref
    b = pl.program_id(1)
    km_ref[b] = (jnp.sum(k0_ref[...], axis=0) + jnp.sum(k1_ref[...], axis=0)) * (1.0 / MOBA_BLOCK)

    @pl.when(b == nblk - 1)
    def _():
        gate = jnp.sum(km_ref[...] * q_ref[...][None], axis=-1, keepdims=True)
        bidx = lax.broadcasted_iota(jnp.int32, gate.shape, 0)
        rank = jnp.zeros(gate.shape, jnp.int32)
        for b2 in range(nblk):
            gb = gate[b2:b2 + 1]
            rank = rank + ((gb > gate) | ((gb == gate) & (b2 < bidx))).astype(jnp.int32)
        for s in range(MOBA_TOPK):
            sel_ref[s] = jnp.sum(jnp.where(rank == s, bidx, 0), axis=0)


def _moba_sample_select(cache_k, layer, page_table, q):
    n, h, dh = q.shape
    n_pages = page_table.shape[1]
    pages_per_blk = MOBA_BLOCK // PAGE_SIZE
    assert pages_per_blk == 2
    nblk = n_pages // pages_per_blk
    assert nblk >= MOBA_TOPK
    page_spec = lambda which: pl.BlockSpec(
        (None, None, PAGE_SIZE, h, dh),
        lambda ni, b, pt: (layer, pt[ni, pages_per_blk * b + which], 0, 0, 0))
    sel = pl.pallas_call(
        functools.partial(_moba_select_kernel, nblk=nblk),
        grid_spec=pltpu.PrefetchScalarGridSpec(
            num_scalar_prefetch=1,
            grid=(n, nblk),
            in_specs=[page_spec(0), page_spec(1),
                      pl.BlockSpec((None, h, dh), lambda ni, b, pt: (ni, 0, 0))],
            out_specs=pl.BlockSpec((None, MOBA_TOPK, h, 1), lambda ni, b, pt: (ni, 0, 0, 0)),
            scratch_shapes=[pltpu.VMEM((nblk, h, dh), F32)]),
        out_shape=jax.ShapeDtypeStruct((n, MOBA_TOPK, h, 1), jnp.int32),
        compiler_params=_cparams("parallel", "arbitrary"),
        name="moba_sample_select",
    )(page_table, cache_k, cache_k, q)
    return sel.reshape(n, MOBA_TOPK, h)


def _moba_attend_kernel(pt_ref, sel_ref, ck_hbm, cv_hbm, q_ref, kn_ref, vn_ref, o_ref,
                        kbuf, vbuf, sem, *, layer, heads):
    ni = pl.program_id(0)
    pages_per_blk = MOBA_BLOCK // PAGE_SIZE
    n_slices = MOBA_TOPK * pages_per_blk
    scale = HEAD_DIM ** -0.5

    def slice_copies(h, j):
        blk_id = sel_ref[ni, (j // pages_per_blk) * heads + h]
        page = pt_ref[ni, blk_id * pages_per_blk + j % pages_per_blk]
        rows = pl.ds(j * PAGE_SIZE, PAGE_SIZE)
        return (pltpu.make_async_copy(ck_hbm.at[layer, page, :, h, :], kbuf.at[h, rows, :], sem.at[0]),
                pltpu.make_async_copy(cv_hbm.at[layer, page, :, h, :], vbuf.at[h, rows, :], sem.at[1]))

    for h in range(heads):
        for j in range(n_slices):
            for cp in slice_copies(h, j):
                cp.start()
    for h in range(heads):
        for j in range(n_slices):
            for cp in slice_copies(h, j):
                cp.wait()

    for h in range(heads):
        q = q_ref[h:h + 1, :]
        s_new = jnp.sum(q * kn_ref[h:h + 1, :], axis=-1, keepdims=True) * scale
        s = jnp.sum(kbuf[h] * q, axis=-1, keepdims=True) * scale
        m = jnp.maximum(jnp.max(s, axis=0, keepdims=True), s_new)
        p = jnp.exp(s - m)
        p_new = jnp.exp(s_new - m)
        l = jnp.sum(p, axis=0, keepdims=True) + p_new
        o = jnp.sum(p * vbuf[h], axis=0, keepdims=True) + p_new * vn_ref[h:h + 1, :]
        o_ref[h:h + 1, :] = o / l


def _moba_sample_attend(cache_k, cache_v, layer, page_table, sel, q, k_new, v_new):
    n, h, dh = q.shape
    n_keys = MOBA_TOPK * MOBA_BLOCK
    vec_spec = pl.BlockSpec((None, h, dh), lambda ni, pt, sl: (ni, 0, 0))
    any_spec = pl.BlockSpec(memory_space=pl.ANY)
    return pl.pallas_call(
        functools.partial(_moba_attend_kernel, layer=layer, heads=h),
        grid_spec=pltpu.PrefetchScalarGridSpec(
            num_scalar_prefetch=2,
            grid=(n,),
            in_specs=[any_spec, any_spec, vec_spec, vec_spec, vec_spec],
            out_specs=vec_spec,
            scratch_shapes=[pltpu.VMEM((h, n_keys, dh), F32), pltpu.VMEM((h, n_keys, dh), F32),
                            pltpu.SemaphoreType.DMA((2,))]),
        out_shape=jax.ShapeDtypeStruct((n, h, dh), F32),
        compiler_params=_cparams("arbitrary"),
        name="moba_sample_attend",
    )(page_table, sel.reshape(n, MOBA_TOPK * h), cache_k, cache_v, q, k_new, v_new)


def _sb_sample_kernel(pt_ref, k_ref, v_ref, q_ref, o_ref, c_ref, acc_ref, *, n_pages):
    del pt_ref
    j = pl.program_id(1)
    scale = HEAD_DIM ** -0.5

    @pl.when(j == 0)
    def _():
        c_ref[...] = jnp.zeros(c_ref.shape, F32)
        acc_ref[...] = jnp.zeros(acc_ref.shape, F32)

    z = jnp.sum(k_ref[...] * q_ref[...][None], axis=-1, keepdims=True) * scale
    ls_pos, lk = _log_sigmoid_pair(z)
    incl = lk
    shift = 1
    while shift < PAGE_SIZE:
        incl = incl + jnp.concatenate(
            [incl[shift:], jnp.zeros((shift,) + incl.shape[1:], F32)], axis=0)
        shift *= 2
    a = jnp.exp(ls_pos + (incl - lk) + c_ref[...][None])
    acc_ref[...] += jnp.sum(a * v_ref[...], axis=0)
    c_ref[...] += incl[0]

    @pl.when(j == n_pages - 1)
    def _():
        o_ref[...] = acc_ref[...]


def _sb_sample(cache_k, cache_v, layer, page_table, q):
    n, h, dh = q.shape
    n_pages = page_table.shape[1]
    page_spec = pl.BlockSpec((None, None, PAGE_SIZE, h, dh),
                             lambda ni, j, pt: (layer, pt[ni, n_pages - 1 - j], 0, 0, 0))
    vec_spec = pl.BlockSpec((None, h, dh), lambda ni, j, pt: (ni, 0, 0))
    return pl.pallas_call(
        functools.partial(_sb_sample_kernel, n_pages=n_pages),
        grid_spec=pltpu.PrefetchScalarGridSpec(
            num_scalar_prefetch=1,
            grid=(n, n_pages),
            in_specs=[page_spec, page_spec, vec_spec],
            out_specs=vec_spec,
            scratch_shapes=[pltpu.VMEM((h, 1), F32), pltpu.VMEM((h, dh), F32)]),
        out_shape=jax.ShapeDtypeStruct((n, h, dh), F32),
        compiler_params=_cparams("parallel", "arbitrary"),
        name="sb_sample",
    )(page_table, cache_k, cache_v, q)


def _trunk(x3, pos, w, *, mem=None, mem_k=None, mem_v=None, paged=None, conv_state=None):
    n, t, d = x3.shape
    m = n * t
    depth = w["g_mix"].shape[0]
    hd = w["w_out"].shape[1] // 2
    xw = w["w_xq"].shape[2]
    d_ff = w["w_ffn_gate"].shape[2]
    act = BF16 if t > 1 else F32
    dual = dict(tm=1024, tn=256)
    tables = _rope_tables(pos)
    if t == 1:
        tables = tuple(jnp.broadcast_to(tb, (n, HEAD_DIM)) for tb in tables)
    x = x3.reshape(m, d)
    nka, nva, nkb, nvb, nmk, nmv, nconv = [], [], [], [], [], [], []
    for l in range(depth):
        h = _rmsnorm(x, w["g_mix"][l], act)
        if l % 2 == 0:
            i = l // 2
            proj = lambda j, dt: _mm(h, w["w_in"], i, n=hd, col_off=j * hd, out_dtype=dt)
            qa, ka, va = proj(0, F32), proj(1, F32), proj(2, F32)
            qb, kb, vb = proj(3, act), proj(4, F32), proj(5, F32)
            rot_len = t if t > 1 else n
            qa = _rotary(qa, tables, rot_len)
            ka = _rotary(ka, tables, rot_len)
            nka.append(ka); nva.append(va); nkb.append(kb); nvb.append(vb)
            if paged is None:
                r3 = lambda a: a.reshape(n, t, hd)
                oa = _moba_prompt(r3(qa), r3(ka), r3(va)).reshape(m, hd)
                ob = _sb_prompt(r3(qb), r3(kb), r3(vb)).reshape(m, hd)
            else:
                ck_a, cv_a, ck_b, cv_b, table = paged
                heads = hd // HEAD_DIM
                rh = lambda a: a.reshape(n, heads, HEAD_DIM)
                sel = _moba_sample_select(ck_a, i, table, rh(qa))
                oa = _moba_sample_attend(ck_a, cv_a, i, table, sel, rh(qa), rh(ka), rh(va)).reshape(m, hd)
                ob = _sb_sample(ck_b, cv_b, i, table, rh(qb)).reshape(m, hd)
            x = _mm(oa, w["w_out"], i, n=d, k=hd, row_off=0, residual=x, tk=hd)
            x = _mm(ob, w["w_out"], i, n=d, k=hd, row_off=hd, residual=x, tk=hd)
        else:
            j = l // 2
            u = _mm(h, w["w_pw1"], j, w2=w["w_pw1"], n=d, col_off=0, col_off2=d, epilogue="glu", **dual)
            if conv_state is None:
                prefix = jnp.zeros((n, CONV_PAD, d), F32)
                y = _conv_prompt(u.reshape(n, t, d), prefix, w["w_dw"][j], w["b_dw"][j],
                                 w["g_conv_ln"][j], w["b_conv_ln"][j]).reshape(m, d)
                nconv.append(u.reshape(n, t, d)[:, t - (CONV_W - 1):])
            else:
                st = conv_state[j]
                y = _conv_sample(st, u, w["w_dw"][j], w["b_dw"][j], w["g_conv_ln"][j], w["b_conv_ln"][j])
                nconv.append(jnp.concatenate([st, u[:, None, :]], 1)[:, 1:])
            x = _mm(y, w["w_pw2"], j, n=d, residual=x)
        if mem is not None:
            mn = _rmsnorm(mem.reshape(-1, d), w["g_mem"][l], BF16)
            mk = _mm(mn, w["w_xkv"], l, n=xw, col_off=0)
            mv = _mm(mn, w["w_xkv"], l, n=xw, col_off=xw)
            mem_len = mem.shape[1]
            nmk.append(mk.reshape(n, mem_len, X_HEADS, X_HEAD_DIM))
            nmv.append(mv.reshape(n, mem_len, X_HEADS, X_HEAD_DIM))
            mk4, mv4, lyr = mk.reshape(1, n, mem_len, xw), mv.reshape(1, n, mem_len, xw), 0
        else:
            mk4 = mem_k.reshape(mem_k.shape[0], n, mem_k.shape[2], xw)
            mv4 = mem_v.reshape(mem_v.shape[0], n, mem_v.shape[2], xw)
            lyr = l
        hx = _rmsnorm(x, w["g_xattn"][l], act)
        q = _mm(hx, w["w_xq"], l, n=xw, out_dtype=act)
        o = _xattn(q.reshape(n, t, xw), mk4, mv4, lyr, act).reshape(m, xw)
        x = _mm(o, w["w_xo"], l, n=d, residual=x)
        hf = _rmsnorm(x, w["g_ffn"][l], act)
        g = _mm(hf, w["w_ffn_gate"], l, w2=w["w_ffn_up"], n=d_ff, epilogue="swiglu", out_dtype=act, **dual)
        x = _mm(g, w["w_ffn_down_bf16"], l, n=d, residual=x, tk=d_ff // 2)
    y = _rmsnorm(x, w["g_final"], F32).reshape(n, t, d)
    return y, nka, nva, nkb, nvb, nmk, nmv, nconv


def kernel(x_prompt, x_sample, cache_k_a, cache_v_a, cache_k_b, cache_v_b, cache_mem_k, cache_mem_v,
           state_conv, page_table, mem_prompt, g_mix, w_in, w_out, w_pw1, w_dw, b_dw, g_conv_ln,
           b_conv_ln, w_pw2, g_mem, g_xattn, w_xq, w_xkv, w_xo, g_ffn, w_ffn_gate, w_ffn_up,
           w_ffn_down, g_final):
    w = {"g_mix": g_mix, "w_in": w_in, "w_out": w_out, "w_pw1": w_pw1, "w_dw": w_dw,
         "b_dw": b_dw, "g_conv_ln": g_conv_ln, "b_conv_ln": b_conv_ln, "w_pw2": w_pw2,
         "g_mem": g_mem, "g_xattn": g_xattn, "w_xq": w_xq, "w_xkv": w_xkv,
         "w_xo": w_xo, "g_ffn": g_ffn, "w_ffn_gate": w_ffn_gate, "w_ffn_up": w_ffn_up,
         "w_ffn_down_bf16": w_ffn_down.astype(BF16), "g_final": g_final}
    nb, t, _ = x_prompt.shape
    ns, ts, _ = x_sample.shape
    past_len = page_table.shape[1] * PAGE_SIZE
    pos_p = jnp.arange(t, dtype=jnp.int32)
    pos_s = past_len + jnp.arange(ts, dtype=jnp.int32)
    heads = w_out.shape[1] // 2 // HEAD_DIM

    y_p, ka_p, va_p, kb_p, vb_p, mk_p, mv_p, cv_p = _trunk(x_prompt, pos_p, w, mem=mem_prompt)
    y_s, ka_s, va_s, kb_s, vb_s, _, _, cv_s = _trunk(
        x_sample, pos_s, w, mem_k=cache_mem_k, mem_v=cache_mem_v,
        paged=(cache_k_a, cache_v_a, cache_k_b, cache_v_b, page_table), conv_state=state_conv)

    def heads_p(xs):
        return jnp.stack([a.reshape(nb, t, heads, HEAD_DIM) for a in xs])

    def heads_s(xs):
        return jnp.stack([a.reshape(ns, ts, heads, HEAD_DIM) for a in xs])

    return (y_p, y_s,
            heads_p(ka_p), heads_p(va_p), heads_p(kb_p), heads_p(vb_p),
            jnp.stack(mk_p), jnp.stack(mv_p), jnp.stack(cv_p),
            heads_s(ka_s), heads_s(va_s), heads_s(kb_s), heads_s(vb_s), jnp.stack(cv_s))
```

```python
import functools

import jax
import jax.numpy as jnp
from jax import lax
from jax.experimental import pallas as pl
from jax.experimental.pallas import tpu as pltpu

F32 = jnp.float32
BF16 = jnp.bfloat16

HEAD_DIM = 128
ROT_DIM = HEAD_DIM // 4
ROPE_THETA = 500000.0
MOBA_BLOCK = 256
MOBA_TOPK = 3
CONV_W = 31
X_HEADS = 4
X_HEAD_DIM = 128
EPS = 1e-6
PAGE_SIZE = 128

V7X_VMEM_LIMIT_BYTES = 60 * 1024 * 1024
SUBLANES = 8
NEG_BIG = -1e30
CONV_PAD = 32


def _cparams(*sem):
    return pltpu.CompilerParams(dimension_semantics=sem, vmem_limit_bytes=V7X_VMEM_LIMIT_BYTES)


def _sigmoid(x):
    return 1.0 / (1.0 + jnp.exp(-x))


def _rmsnorm_kernel(x_ref, g_ref, o_ref):
    x = x_ref[...]
    ms = jnp.mean(x * x, axis=-1, keepdims=True)
    o_ref[...] = (x * lax.rsqrt(ms + EPS) * g_ref[...]).astype(o_ref.dtype)


def _rmsnorm(x, g, out_dtype, tm=256):
    m, d = x.shape
    tm = min(tm, m)
    return pl.pallas_call(
        _rmsnorm_kernel,
        grid=(m // tm,),
        in_specs=[pl.BlockSpec((tm, d), lambda i: (i, 0)),
                  pl.BlockSpec((1, d), lambda i: (0, 0))],
        out_specs=pl.BlockSpec((tm, d), lambda i: (i, 0)),
        out_shape=jax.ShapeDtypeStruct((m, d), out_dtype),
        compiler_params=_cparams("parallel"),
        name="rmsnorm",
    )(x, g.reshape(1, d))


def _rope_rotate(x, c, s1, s2):
    half = ROT_DIM // 2
    return x * c + pltpu.roll(x, HEAD_DIM - half, 1) * s1 + pltpu.roll(x, half, 1) * s2


def _mm_kernel(*refs, nk, n_w, has_res, has_rope, epilogue):
    a_ref = refs[0]
    w_refs = refs[1:1 + n_w]
    pos = 1 + n_w
    r_ref = refs[pos] if has_res else None
    pos += int(has_res)
    rope_refs = refs[pos:pos + 3] if has_rope else None
    pos += 3 * int(has_rope)
    o_ref = refs[pos]
    acc_refs = refs[pos + 1:]

    a = a_ref[...].astype(BF16)
    ps = [jnp.dot(a, w_ref[...].astype(BF16), preferred_element_type=F32) for w_ref in w_refs]

    def finish(vals):
        if epilogue == "swiglu":
            y = vals[0] * _sigmoid(vals[0]) * vals[1]
        elif epilogue == "glu":
            y = vals[0] * _sigmoid(vals[1])
        else:
            y = vals[0]
        if has_res:
            y = r_ref[...] + y
        if has_rope:
            c, s1, s2 = (t[...] for t in rope_refs)
            for h in range(y.shape[1] // HEAD_DIM):
                sl = slice(h * HEAD_DIM, (h + 1) * HEAD_DIM)
                o_ref[:, sl] = _rope_rotate(y[:, sl], c, s1, s2).astype(o_ref.dtype)
        else:
            o_ref[...] = y.astype(o_ref.dtype)

    if nk == 1:
        finish(ps)
    else:
        kk = pl.program_id(2)

        @pl.when(kk == 0)
        def _():
            for acc, p in zip(acc_refs, ps):
                acc[...] = p

        @pl.when(kk > 0)
        def _():
            for acc, p in zip(acc_refs, ps):
                acc[...] += p

        @pl.when(kk == nk - 1)
        def _():
            finish([acc[...] for acc in acc_refs])


def _mm(a, w, layer, *, n, k=None, col_off=0, row_off=0, w2=None, col_off2=0, epilogue="none",
        residual=None, rope=None, out_dtype=F32, tm=1024, tn=512, tk=None):
    m, ka = a.shape
    k = ka if k is None else k
    tm = min(tm, m)
    tn = min(tn, n)
    tk = k if tk is None else tk
    nk = k // tk
    assert m % tm == 0 and n % tn == 0 and k % tk == 0
    assert col_off % tn == 0 and col_off2 % tn == 0 and row_off % tk == 0
    cb, cb2, rb = col_off // tn, col_off2 // tn, row_off // tk
    ws = [w] if w2 is None else [w, w2]
    cbs = [cb, cb2]
    in_specs = [pl.BlockSpec((tm, tk), lambda i, j, kk: (i, kk))]
    for idx in range(len(ws)):
        in_specs.append(pl.BlockSpec((None, tk, tn), functools.partial(
            lambda i, j, kk, c: (layer, rb + kk, c + j), c=cbs[idx])))
    args = [a] + ws
    if residual is not None:
        in_specs.append(pl.BlockSpec((tm, tn), lambda i, j, kk: (i, j)))
        args.append(residual)
    if rope is not None:
        tables, t_len = rope
        assert t_len % tm == 0 and tn % HEAD_DIM == 0
        nt = t_len // tm
        in_specs += [pl.BlockSpec((tm, HEAD_DIM), lambda i, j, kk: (i % nt, 0))] * 3
        args += list(tables)
    scratch = [pltpu.VMEM((tm, tn), F32) for _ in ws] if nk > 1 else []
    return pl.pallas_call(
        functools.partial(_mm_kernel, nk=nk, n_w=len(ws), has_res=residual is not None,
                          has_rope=rope is not None, epilogue=epilogue),
        grid=(m // tm, n // tn, nk),
        in_specs=in_specs,
        out_specs=pl.BlockSpec((tm, tn), lambda i, j, kk: (i, j)),
        out_shape=jax.ShapeDtypeStruct((m, n), out_dtype),
        scratch_shapes=scratch,
        compiler_params=_cparams("parallel", "parallel", "arbitrary"),
        name="mm_" + epilogue,
    )(*args)


def _mm_res_norm_kernel(a_ref, w_ref, r_ref, g_ref, x_ref, h_ref):
    y = r_ref[...] + jnp.dot(a_ref[...].astype(BF16), w_ref[...].astype(BF16), preferred_element_type=F32)
    x_ref[...] = y
    ms = jnp.mean(y * y, axis=-1, keepdims=True)
    h_ref[...] = (y * lax.rsqrt(ms + EPS) * g_ref[...]).astype(h_ref.dtype)


def _mm_res_norm(a, w, layer, residual, g, h_dtype, tm=256):
    m, k = a.shape
    n = w.shape[2]
    tm = min(tm, m)
    row = lambda width: pl.BlockSpec((tm, width), lambda i: (i, 0))
    return pl.pallas_call(
        _mm_res_norm_kernel,
        grid=(m // tm,),
        in_specs=[row(k), pl.BlockSpec((None, k, n), lambda i: (layer, 0, 0)), row(n),
                  pl.BlockSpec((1, n), lambda i: (0, 0))],
        out_specs=[row(n), row(n)],
        out_shape=[jax.ShapeDtypeStruct((m, n), F32), jax.ShapeDtypeStruct((m, n), h_dtype)],
        compiler_params=_cparams("parallel"),
        name="mm_res_norm",
    )(a, w, residual, g.reshape(1, n))


def _rope_tables(pos):
    half = ROT_DIM // 2
    inv = ROPE_THETA ** (-jnp.arange(0, ROT_DIM, 2, dtype=F32) / ROT_DIM)
    ang = pos.astype(F32)[:, None] * inv[None, :]
    cos, sin = jnp.cos(ang), jnp.sin(ang)
    t = pos.shape[0]
    rest = HEAD_DIM - ROT_DIM
    c = jnp.concatenate([cos, cos, jnp.ones((t, rest), F32)], 1)
    s1 = jnp.concatenate([-sin, jnp.zeros((t, half + rest), F32)], 1)
    s2 = jnp.concatenate([jnp.zeros((t, half), F32), sin, jnp.zeros((t, rest), F32)], 1)
    return c, s1, s2


def _moba_tile(c, q_ref, o_ref, k16_ref, vt_ref, kmean_ref, pen_ref, *, nb):
    blk = MOBA_BLOCK
    scale = HEAD_DIM ** -0.5
    q_t = q_ref[...].T
    gate = lax.dot_general(kmean_ref[...], q_t, (((1,), (0,)), ((), ())),
                           precision=lax.Precision.HIGHEST, preferred_element_type=F32)
    bidx = lax.broadcasted_iota(jnp.int32, (nb, blk), 0)
    valid = bidx < c
    gm = jnp.where(valid, gate, -jnp.inf)
    rank = jnp.zeros((nb, blk), jnp.int32)
    for b2 in range(c):
        gb = gm[b2:b2 + 1, :]
        rank = rank + ((gb > gm) | ((gb == gm) & (b2 < bidx))).astype(jnp.int32)
    pen_ref[...] = jnp.where(valid & (rank < MOBA_TOPK), 0.0, NEG_BIG)

    q16_t = q_t.astype(BF16)
    krow = lax.broadcasted_iota(jnp.int32, (blk, blk), 0)
    qcol = lax.broadcasted_iota(jnp.int32, (blk, blk), 1)
    s_blocks = []
    for b in range(c + 1):
        s = jnp.dot(k16_ref[b * blk:(b + 1) * blk, :], q16_t, preferred_element_type=F32) * scale
        if b == c:
            s_blocks.append(jnp.where(krow <= qcol, s, NEG_BIG))
        else:
            s_blocks.append(s + pen_ref[b:b + 1, :])
    m = s_blocks[0]
    for s in s_blocks[1:]:
        m = jnp.maximum(m, s)
    m = jnp.max(m, axis=0, keepdims=True)
    p_blocks = [jnp.exp(s - m) for s in s_blocks]
    psum = p_blocks[0]
    for p in p_blocks[1:]:
        psum = psum + p
    l = jnp.sum(psum, axis=0, keepdims=True)
    p_all = jnp.concatenate([p.astype(BF16) for p in p_blocks], axis=0)
    o_t = jnp.dot(vt_ref[:, 0:(c + 1) * blk], p_all, preferred_element_type=F32)
    o_ref[...] = (o_t / l).T.astype(o_ref.dtype)


def _moba_kernel(q_ref, k_ref, v_ref, o_ref, k16_ref, vt_ref, kmean_ref, pen0_ref, pen1_ref, *, nb):
    blk = MOBA_BLOCK
    k = k_ref[...]
    kmean_ref[...] = jnp.mean(k.reshape(nb, blk, HEAD_DIM), axis=1)
    k16_ref[...] = k.astype(BF16)
    vt_ref[...] = v_ref[...].T.astype(BF16)
    for c in range(nb):
        rows = slice(c * blk, (c + 1) * blk)
        _moba_tile(c, q_ref.at[rows, :], o_ref.at[rows, :], k16_ref, vt_ref, kmean_ref,
                   pen1_ref if c % 2 else pen0_ref, nb=nb)


def _moba_prompt(q, k, v):
    b, t, w = q.shape
    h = w // HEAD_DIM
    blk = MOBA_BLOCK
    nb = max(-(-t // blk), MOBA_TOPK + 1)
    assert t % blk == 0 and nb * blk == t
    spec = pl.BlockSpec((None, t, HEAD_DIM), lambda bi, hi: (bi, 0, hi))
    return pl.pallas_call(
        functools.partial(_moba_kernel, nb=nb),
        grid=(b, h),
        in_specs=[spec, spec, spec],
        out_specs=spec,
        out_shape=jax.ShapeDtypeStruct((b, t, w), BF16),
        scratch_shapes=[pltpu.VMEM((t, HEAD_DIM), BF16), pltpu.VMEM((HEAD_DIM, t), BF16),
                        pltpu.VMEM((nb, HEAD_DIM), F32), pltpu.VMEM((nb, blk), F32),
                        pltpu.VMEM((nb, blk), F32)],
        compiler_params=_cparams("parallel", "parallel"),
        name="moba_prompt",
    )(q, k, v)


SB_BLOCK = 256


def _log_sigmoid_pair(z):
    soft = jnp.log(1.0 + jnp.exp(-jnp.abs(z)))
    ls_pos = jnp.minimum(z, 0.0) - soft
    return ls_pos, ls_pos - z


def _sb_tile(c, q_ref, o_ref, k16_ref, vt_ref, z_ref, pre_ref, hi_ref, lo_ref, a_ref, tot_ref):
    blk = SB_BLOCK
    scale = HEAD_DIM ** -0.5
    n_keys = (c + 1) * blk
    q16_t = q_ref[...].astype(F32).T.astype(BF16)
    krow = lax.broadcasted_iota(jnp.int32, (blk, blk), 0)
    qcol = lax.broadcasted_iota(jnp.int32, (blk, blk), 1)
    strictly_past = krow < qcol
    later = (qcol > krow).astype(BF16)
    blocks = [slice(kb * blk, (kb + 1) * blk) for kb in range(c + 1)]

    z_ref[0:n_keys, :] = jnp.dot(k16_ref[0:n_keys, :], q16_t, preferred_element_type=F32)
    for kb, ks in enumerate(blocks):
        ls_pos, lk = _log_sigmoid_pair(z_ref[ks, :] * scale)
        if kb == c:
            lk = jnp.where(strictly_past, lk, 0.0)
        lk_hi = lk.astype(BF16)
        hi_ref[ks, :] = lk_hi
        lo_ref[ks, :] = (lk - lk_hi.astype(F32)).astype(BF16)
        pre_ref[ks, :] = ls_pos
        tot_ref[kb:kb + 1, :] = lk[0:1, :]
    for kb, ks in enumerate(blocks):
        after = (jnp.dot(later, hi_ref[ks, :], preferred_element_type=F32)
                 + jnp.dot(later, lo_ref[ks, :], preferred_element_type=F32))
        pre_ref[ks, :] += after
        tot_ref[kb:kb + 1, :] += after[0:1, :]
    carry = jnp.zeros((1, blk), F32)
    for kb in range(c, -1, -1):
        ks = blocks[kb]
        a = jnp.exp(pre_ref[ks, :] + carry)
        if kb == c:
            a = jnp.where(strictly_past, a, 0.0)
        a_ref[ks, :] = a.astype(BF16)
        carry = carry + tot_ref[kb:kb + 1, :]
    o_t = jnp.dot(vt_ref[:, 0:n_keys], a_ref[0:n_keys, :], preferred_element_type=F32)
    o_ref[...] = o_t.T.astype(o_ref.dtype)


def _sb_kernel(q_ref, k_ref, v_ref, o_ref, k16_ref, vt_ref, *stage_refs, nq):
    blk = SB_BLOCK
    k16_ref[...] = k_ref[...].astype(BF16)
    vt_ref[...] = v_ref[...].T.astype(BF16)
    half = len(stage_refs) // 2
    for c in range(nq):
        rows = slice(c * blk, (c + 1) * blk)
        _sb_tile(c, q_ref.at[rows, :], o_ref.at[rows, :], k16_ref, vt_ref,
                 *stage_refs[(c % 2) * half:(c % 2 + 1) * half])


def _sb_prompt(q, k, v):
    b, t, w = q.shape
    h = w // HEAD_DIM
    blk = SB_BLOCK
    assert t % blk == 0
    spec = pl.BlockSpec((None, t, HEAD_DIM), lambda bi, hi: (bi, 0, hi))
    stage = [pltpu.VMEM((t, blk), F32), pltpu.VMEM((t, blk), F32),
             pltpu.VMEM((t, blk), BF16), pltpu.VMEM((t, blk), BF16),
             pltpu.VMEM((t, blk), BF16), pltpu.VMEM((t // blk, blk), F32)]
    return pl.pallas_call(
        functools.partial(_sb_kernel, nq=t // blk),
        grid=(b, h),
        in_specs=[spec, spec, spec],
        out_specs=spec,
        out_shape=jax.ShapeDtypeStruct((b, t, w), BF16),
        scratch_shapes=[pltpu.VMEM((t, HEAD_DIM), BF16), pltpu.VMEM((HEAD_DIM, t), BF16)] + stage + stage,
        compiler_params=_cparams("parallel", "parallel"),
        name="sb_prompt",
    )(q, k, v)


CONV_LANES = 256
CONV_LEAD = CONV_PAD - (CONV_W - 1)


def _layernorm_silu(y, g, b):
    mu = jnp.mean(y, axis=-1, keepdims=True)
    yc = y - mu
    var = jnp.mean(yc * yc, axis=-1, keepdims=True)
    yn = yc * lax.rsqrt(var + EPS) * g + b
    return yn * _sigmoid(yn)


def _conv_tap_groups():
    return [[CONV_LEAD + w for w in range(CONV_W) if (CONV_LEAD + w) % SUBLANES == res]
            for res in range(SUBLANES)]


def _conv_kernel(cur_ref, tail_ref, pre_ref, w_ref, bdw_ref, g_ref, b_ref, o_ref, win_ref, y_ref, sh_ref, *, tt):
    i = pl.program_id(1)
    d = cur_ref.shape[-1]

    @pl.when(i == 0)
    def _():
        win_ref[0:CONV_PAD, :] = pre_ref[...]

    @pl.when(i > 0)
    def _():
        win_ref[0:CONV_PAD, :] = tail_ref[...]

    win_ref[CONV_PAD:, :] = cur_ref[...]
    groups = _conv_tap_groups()
    for c in range(d // CONV_LANES):
        ls = slice(c * CONV_LANES, (c + 1) * CONV_LANES)
        buf = c % 2
        for res, offs in enumerate(groups):
            n = offs[-1] - offs[0] + tt
            sh_ref[buf, res, 0:n, :] = win_ref[offs[0]:offs[0] + n, ls]
        acc = jnp.zeros((tt, CONV_LANES), F32)
        for res, offs in enumerate(groups):
            for off in offs:
                w = off - CONV_LEAD
                acc = acc + w_ref[w:w + 1, ls] * sh_ref[buf, res, off - offs[0]:off - offs[0] + tt, :]
        y_ref[:, ls] = acc + bdw_ref[:, ls]
    o_ref[...] = _layernorm_silu(y_ref[...], g_ref[...], b_ref[...]).astype(o_ref.dtype)


def _conv_prompt(u, prefix, w_dw, b_dw, g_ln, b_ln, tt=128):
    b, t, d = u.shape
    assert t % tt == 0 and tt % CONV_PAD == 0 and d % CONV_LANES == 0
    per = tt // CONV_PAD
    w_pad = jnp.concatenate([w_dw, jnp.zeros((CONV_PAD - CONV_W, d), F32)], 0)
    vec = lambda x: x.reshape(1, d)
    vspec = pl.BlockSpec((1, d), lambda bi, i: (0, 0))
    shifted_rows = tt + CONV_PAD - SUBLANES
    return pl.pallas_call(
        functools.partial(_conv_kernel, tt=tt),
        grid=(b, t // tt),
        in_specs=[pl.BlockSpec((None, tt, d), lambda bi, i: (bi, i, 0)),
                  pl.BlockSpec((None, CONV_PAD, d), lambda bi, i: (bi, jnp.maximum(i * per - 1, 0), 0)),
                  pl.BlockSpec((None, CONV_PAD, d), lambda bi, i: (bi, 0, 0)),
                  pl.BlockSpec((CONV_PAD, d), lambda bi, i: (0, 0)),
                  vspec, vspec, vspec],
        out_specs=pl.BlockSpec((None, tt, d), lambda bi, i: (bi, i, 0)),
        out_shape=jax.ShapeDtypeStruct((b, t, d), BF16),
        scratch_shapes=[pltpu.VMEM((tt + CONV_PAD, d), F32), pltpu.VMEM((tt, d), F32),
                        pltpu.VMEM((2, SUBLANES, shifted_rows, CONV_LANES), F32)],
        compiler_params=_cparams("parallel", "arbitrary"),
        name="conv_prompt",
    )(u, u, prefix, w_pad, vec(b_dw), vec(g_ln), vec(b_ln))


def _conv_sample_kernel(st_ref, u_ref, w_ref, bdw_ref, g_ref, b_ref, o_ref):
    n = st_ref.shape[0]
    for s in range(n):
        st = st_ref[s]
        y = jnp.sum(st * w_ref[0:CONV_W - 1, :], axis=0, keepdims=True)
        y = y + u_ref[s:s + 1, :] * w_ref[CONV_W - 1:CONV_W, :] + bdw_ref[...]
        o_ref[s:s + 1, :] = _layernorm_silu(y, g_ref[...], b_ref[...]).astype(o_ref.dtype)


def _conv_sample(state, u, w_dw, b_dw, g_ln, b_ln):
    n, _, d = state.shape
    vec = lambda x: x.reshape(1, d)
    return pl.pallas_call(
        _conv_sample_kernel,
        out_shape=jax.ShapeDtypeStruct((n, d), F32),
        compiler_params=pltpu.CompilerParams(vmem_limit_bytes=V7X_VMEM_LIMIT_BYTES),
        name="conv_sample",
    )(state, u, w_dw, vec(b_dw), vec(g_ln), vec(b_ln))


def _xattn_kernel(q_ref, k_ref, v_ref, o_ref, *, tq):
    scale = X_HEAD_DIM ** -0.5
    for h in range(X_HEADS):
        sl = slice(h * X_HEAD_DIM, (h + 1) * X_HEAD_DIM)
        if tq == 1:
            q = q_ref[:, sl].astype(F32)
            s = jnp.sum(k_ref[:, sl] * q, axis=-1, keepdims=True) * scale
            p = jnp.exp(s - jnp.max(s, axis=0, keepdims=True))
            p = p / jnp.sum(p, axis=0, keepdims=True)
            o = jnp.sum(p * v_ref[:, sl], axis=0, keepdims=True)
        else:
            q = q_ref[:, sl].astype(BF16)
            s = lax.dot_general(q, k_ref[:, sl].astype(BF16), (((1,), (1,)), ((), ())),
                                preferred_element_type=F32) * scale
            p = jnp.exp(s - jnp.max(s, axis=1, keepdims=True))
            p = p / jnp.sum(p, axis=1, keepdims=True)
            o = jnp.dot(p.astype(BF16), v_ref[:, sl].astype(BF16), preferred_element_type=F32)
        o_ref[:, sl] = o.astype(o_ref.dtype)


def _xattn(q, mk, mv, layer, out_dtype, tq=512):
    n, t, xw = q.shape
    mem = mk.shape[2]
    tq = min(tq, t)
    kv_spec = pl.BlockSpec((None, None, mem, xw), lambda ni, i: (layer, ni, 0, 0))
    q_spec = pl.BlockSpec((None, tq, xw), lambda ni, i: (ni, i, 0))
    return pl.pallas_call(
        functools.partial(_xattn_kernel, tq=tq),
        grid=(n, t // tq),
        in_specs=[q_spec, kv_spec, kv_spec],
        out_specs=q_spec,
        out_shape=jax.ShapeDtypeStruct((n, t, xw), out_dtype),
        compiler_params=_cparams("parallel", "parallel"),
        name="xattn",
    )(q, mk, mv)


SELECT_BLOCKS_PER_STEP = 2


def _moba_select_kernel(pt_ref, *refs, nblk, pages_per_blk):
    del pt_ref
    n_pages = SELECT_BLOCKS_PER_STEP * pages_per_blk
    page_refs, (q_ref, sel_ref, km_ref) = refs[:n_pages], refs[n_pages:]
    step = pl.program_id(1)
    for sb in range(SELECT_BLOCKS_PER_STEP):
        total = jnp.sum(page_refs[sb * pages_per_blk][...], axis=0)
        for pg in range(1, pages_per_blk):
            total = total + jnp.sum(page_refs[sb * pages_per_blk + pg][...], axis=0)
        km_ref[step * SELECT_BLOCKS_PER_STEP + sb] = total * (1.0 / MOBA_BLOCK)

    @pl.when(step == nblk // SELECT_BLOCKS_PER_STEP - 1)
    def _():
        gate = jnp.sum(km_ref[...] * q_ref[...][None], axis=-1, keepdims=True)
        bidx = lax.broadcasted_iota(jnp.int32, gate.shape, 0)
        rank = jnp.zeros(gate.shape, jnp.int32)
        for b2 in range(nblk):
            gb = gate[b2:b2 + 1]
            rank = rank + ((gb > gate) | ((gb == gate) & (b2 < bidx))).astype(jnp.int32)
        for s in range(MOBA_TOPK):
            sel_ref[s] = jnp.sum(jnp.where(rank == s, bidx, 0), axis=0)


def _moba_sample_select(cache_k, layer, page_table, q):
    n, h, dh = q.shape
    n_pages = page_table.shape[1]
    pages_per_blk = MOBA_BLOCK // PAGE_SIZE
    nblk = n_pages // pages_per_blk
    assert nblk >= MOBA_TOPK and nblk % SELECT_BLOCKS_PER_STEP == 0
    pages_per_step = SELECT_BLOCKS_PER_STEP * pages_per_blk
    page_spec = lambda which: pl.BlockSpec(
        (None, None, PAGE_SIZE, h, dh),
        lambda ni, b, pt: (layer, pt[ni, pages_per_step * b + which], 0, 0, 0))
    sel = pl.pallas_call(
        functools.partial(_moba_select_kernel, nblk=nblk, pages_per_blk=pages_per_blk),
        grid_spec=pltpu.PrefetchScalarGridSpec(
            num_scalar_prefetch=1,
            grid=(n, nblk // SELECT_BLOCKS_PER_STEP),
            in_specs=[page_spec(p) for p in range(pages_per_step)]
                     + [pl.BlockSpec((None, h, dh), lambda ni, b, pt: (ni, 0, 0))],
            out_specs=pl.BlockSpec((None, MOBA_TOPK, h, 1), lambda ni, b, pt: (ni, 0, 0, 0)),
            scratch_shapes=[pltpu.VMEM((nblk, h, dh), F32)]),
        out_shape=jax.ShapeDtypeStruct((n, MOBA_TOPK, h, 1), jnp.int32),
        compiler_params=_cparams("parallel", "arbitrary"),
        name="moba_sample_select",
    )(page_table, *([cache_k] * pages_per_step), q)
    return sel.reshape(n, MOBA_TOPK, h)


def _moba_attend_kernel(pt_ref, sel_ref, ck_hbm, cv_hbm, q_ref, kn_ref, vn_ref, o_ref,
                        kbuf, vbuf, sem, *, layer, heads):
    ni = pl.program_id(0)
    pages_per_blk = MOBA_BLOCK // PAGE_SIZE
    n_slices = MOBA_TOPK * pages_per_blk
    scale = HEAD_DIM ** -0.5

    def slice_copies(h, j):
        blk_id = sel_ref[ni, (j // pages_per_blk) * heads + h]
        page = pt_ref[ni, blk_id * pages_per_blk + j % pages_per_blk]
        rows = pl.ds(j * PAGE_SIZE, PAGE_SIZE)
        return (pltpu.make_async_copy(ck_hbm.at[layer, page, :, h, :], kbuf.at[h, rows, :], sem.at[0]),
                pltpu.make_async_copy(cv_hbm.at[layer, page, :, h, :], vbuf.at[h, rows, :], sem.at[1]))

    for h in range(heads):
        for j in range(n_slices):
            for cp in slice_copies(h, j):
                cp.start()
    for h in range(heads):
        for j in range(n_slices):
            for cp in slice_copies(h, j):
                cp.wait()

    for h in range(heads):
        q = q_ref[h:h + 1, :]
        s_new = jnp.sum(q * kn_ref[h:h + 1, :], axis=-1, keepdims=True) * scale
        s = jnp.sum(kbuf[h] * q, axis=-1, keepdims=True) * scale
        m = jnp.maximum(jnp.max(s, axis=0, keepdims=True), s_new)
        p = jnp.exp(s - m)
        p_new = jnp.exp(s_new - m)
        l = jnp.sum(p, axis=0, keepdims=True) + p_new
        o = jnp.sum(p * vbuf[h], axis=0, keepdims=True) + p_new * vn_ref[h:h + 1, :]
        o_ref[h:h + 1, :] = o / l


def _moba_sample_attend(cache_k, cache_v, layer, page_table, sel, q, k_new, v_new):
    n, h, dh = q.shape
    n_keys = MOBA_TOPK * MOBA_BLOCK
    vec_spec = pl.BlockSpec((None, h, dh), lambda ni, pt, sl: (ni, 0, 0))
    any_spec = pl.BlockSpec(memory_space=pl.ANY)
    return pl.pallas_call(
        functools.partial(_moba_attend_kernel, layer=layer, heads=h),
        grid_spec=pltpu.PrefetchScalarGridSpec(
            num_scalar_prefetch=2,
            grid=(n,),
            in_specs=[any_spec, any_spec, vec_spec, vec_spec, vec_spec],
            out_specs=vec_spec,
            scratch_shapes=[pltpu.VMEM((h, n_keys, dh), F32), pltpu.VMEM((h, n_keys, dh), F32),
                            pltpu.SemaphoreType.DMA((2,))]),
        out_shape=jax.ShapeDtypeStruct((n, h, dh), F32),
        compiler_params=_cparams("arbitrary"),
        name="moba_sample_attend",
    )(page_table, sel.reshape(n, MOBA_TOPK * h), cache_k, cache_v, q, k_new, v_new)


def _sb_sample_kernel(pt_ref, k_ref, v_ref, q_ref, o_ref, c_ref, acc_ref, *, n_pages):
    del pt_ref
    j = pl.program_id(1)
    scale = HEAD_DIM ** -0.5

    @pl.when(j == 0)
    def _():
        c_ref[...] = jnp.zeros(c_ref.shape, F32)
        acc_ref[...] = jnp.zeros(acc_ref.shape, F32)

    z = jnp.sum(k_ref[...] * q_ref[...][None], axis=-1, keepdims=True) * scale
    ls_pos, lk = _log_sigmoid_pair(z)
    incl = lk
    shift = 1
    while shift < PAGE_SIZE:
        incl = incl + jnp.concatenate(
            [incl[shift:], jnp.zeros((shift,) + incl.shape[1:], F32)], axis=0)
        shift *= 2
    a = jnp.exp(ls_pos + (incl - lk) + c_ref[...][None])
    acc_ref[...] += jnp.sum(a * v_ref[...], axis=0)
    c_ref[...] += incl[0]

    @pl.when(j == n_pages - 1)
    def _():
        o_ref[...] = acc_ref[...]


def _sb_sample(cache_k, cache_v, layer, page_table, q):
    n, h, dh = q.shape
    n_pages = page_table.shape[1]
    page_spec = pl.BlockSpec((None, None, PAGE_SIZE, h, dh),
                             lambda ni, j, pt: (layer, pt[ni, n_pages - 1 - j], 0, 0, 0))
    vec_spec = pl.BlockSpec((None, h, dh), lambda ni, j, pt: (ni, 0, 0))
    return pl.pallas_call(
        functools.partial(_sb_sample_kernel, n_pages=n_pages),
        grid_spec=pltpu.PrefetchScalarGridSpec(
            num_scalar_prefetch=1,
            grid=(n, n_pages),
            in_specs=[page_spec, page_spec, vec_spec],
            out_specs=vec_spec,
            scratch_shapes=[pltpu.VMEM((h, 1), F32), pltpu.VMEM((h, dh), F32)]),
        out_shape=jax.ShapeDtypeStruct((n, h, dh), F32),
        compiler_params=_cparams("parallel", "arbitrary"),
        name="sb_sample",
    )(page_table, cache_k, cache_v, q)


def _trunk(x3, pos, w, *, mem=None, mem_k=None, mem_v=None, paged=None, conv_state=None):
    n, t, d = x3.shape
    m = n * t
    depth = w["g_mix"].shape[0]
    hd = w["w_out"].shape[1] // 2
    xw = w["w_xq"].shape[2]
    d_ff = w["w_ffn_gate"].shape[2]
    act = BF16 if t > 1 else F32
    dual = dict(tm=1024, tn=256)
    wide = dict(tm=2048, tn=256)
    tables = _rope_tables(pos)
    if t == 1:
        tables = tuple(jnp.broadcast_to(tb, (n, HEAD_DIM)) for tb in tables)
    x = x3.reshape(m, d)
    nka, nva, nkb, nvb, nmk, nmv, nconv = [], [], [], [], [], [], []
    for l in range(depth):
        h = _rmsnorm(x, w["g_mix"][l], act)
        if l % 2 == 0:
            i = l // 2
            rope = (tables, t if t > 1 else n)
            proj = lambda j, dt, rp=None: _mm(h, w["w_in"], i, n=hd, col_off=j * hd, out_dtype=dt,
                                              rope=rp, **wide)
            qa, ka, va = proj(0, F32, rope), proj(1, F32, rope), proj(2, F32)
            qb, kb, vb = proj(3, act), proj(4, F32), proj(5, F32)
            nka.append(ka); nva.append(va); nkb.append(kb); nvb.append(vb)
            if paged is None:
                r3 = lambda a: a.reshape(n, t, hd)
                oa = _moba_prompt(r3(qa), r3(ka), r3(va)).reshape(m, hd)
                ob = _sb_prompt(r3(qb), r3(kb), r3(vb)).reshape(m, hd)
            else:
                ck_a, cv_a, ck_b, cv_b, table = paged
                heads = hd // HEAD_DIM
                rh = lambda a: a.reshape(n, heads, HEAD_DIM)
                sel = _moba_sample_select(ck_a, i, table, rh(qa))
                oa = _moba_sample_attend(ck_a, cv_a, i, table, sel, rh(qa), rh(ka), rh(va)).reshape(m, hd)
                ob = _sb_sample(ck_b, cv_b, i, table, rh(qb)).reshape(m, hd)
            oab = jnp.concatenate([oa, ob], -1)
            x = _mm(oab, w["w_out"], i, n=d, residual=x, **wide)
        else:
            j = l // 2
            u = _mm(h, w["w_pw1"], j, w2=w["w_pw1"], n=d, col_off=0, col_off2=d, epilogue="glu", **dual)
            if conv_state is None:
                prefix = jnp.zeros((n, CONV_PAD, d), F32)
                y = _conv_prompt(u.reshape(n, t, d), prefix, w["w_dw"][j], w["b_dw"][j],
                                 w["g_conv_ln"][j], w["b_conv_ln"][j]).reshape(m, d)
                nconv.append(u.reshape(n, t, d)[:, t - (CONV_W - 1):])
            else:
                st = conv_state[j]
                y = _conv_sample(st, u, w["w_dw"][j], w["b_dw"][j], w["g_conv_ln"][j], w["b_conv_ln"][j])
                nconv.append(jnp.concatenate([st, u[:, None, :]], 1)[:, 1:])
            x = _mm(y, w["w_pw2"], j, n=d, residual=x, **wide)
        if mem is not None:
            mn = _rmsnorm(mem.reshape(-1, d), w["g_mem"][l], BF16)
            mk = _mm(mn, w["w_xkv"], l, n=xw, col_off=0)
            mv = _mm(mn, w["w_xkv"], l, n=xw, col_off=xw)
            mem_len = mem.shape[1]
            nmk.append(mk.reshape(n, mem_len, X_HEADS, X_HEAD_DIM))
            nmv.append(mv.reshape(n, mem_len, X_HEADS, X_HEAD_DIM))
            mk4, mv4, lyr = mk.reshape(1, n, mem_len, xw), mv.reshape(1, n, mem_len, xw), 0
        else:
            mk4 = mem_k.reshape(mem_k.shape[0], n, mem_k.shape[2], xw)
            mv4 = mem_v.reshape(mem_v.shape[0], n, mem_v.shape[2], xw)
            lyr = l
        hx = _rmsnorm(x, w["g_xattn"][l], act)
        q = _mm(hx, w["w_xq"], l, n=xw, out_dtype=act)
        o = _xattn(q.reshape(n, t, xw), mk4, mv4, lyr, act).reshape(m, xw)
        x, hf = _mm_res_norm(o, w["w_xo"], l, x, w["g_ffn"][l], act)
        g = _mm(hf, w["w_ffn_gate"], l, w2=w["w_ffn_up"], n=d_ff, epilogue="swiglu", out_dtype=act, **dual)
        x = _mm(g, w["w_ffn_down_bf16"], l, n=d, residual=x, tk=d_ff // 2)
    y = _rmsnorm(x, w["g_final"], F32).reshape(n, t, d)
    return y, nka, nva, nkb, nvb, nmk, nmv, nconv


def kernel(x_prompt, x_sample, cache_k_a, cache_v_a, cache_k_b, cache_v_b, cache_mem_k, cache_mem_v,
           state_conv, page_table, mem_prompt, g_mix, w_in, w_out, w_pw1, w_dw, b_dw, g_conv_ln,
           b_conv_ln, w_pw2, g_mem, g_xattn, w_xq, w_xkv, w_xo, g_ffn, w_ffn_gate, w_ffn_up,
           w_ffn_down, g_final):
    w = {"g_mix": g_mix, "w_in": w_in, "w_out": w_out, "w_pw1": w_pw1, "w_dw": w_dw,
         "b_dw": b_dw, "g_conv_ln": g_conv_ln, "b_conv_ln": b_conv_ln, "w_pw2": w_pw2,
         "g_mem": g_mem, "g_xattn": g_xattn, "w_xq": w_xq, "w_xkv": w_xkv,
         "w_xo": w_xo, "g_ffn": g_ffn, "w_ffn_gate": w_ffn_gate, "w_ffn_up": w_ffn_up,
         "w_ffn_down_bf16": w_ffn_down.astype(BF16), "g_final": g_final}
    nb, t, _ = x_prompt.shape
    ns, ts, _ = x_sample.shape
    past_len = page_table.shape[1] * PAGE_SIZE
    pos_p = jnp.arange(t, dtype=jnp.int32)
    pos_s = past_len + jnp.arange(ts, dtype=jnp.int32)
    heads = w_out.shape[1] // 2 // HEAD_DIM

    y_p, ka_p, va_p, kb_p, vb_p, mk_p, mv_p, cv_p = _trunk(x_prompt, pos_p, w, mem=mem_prompt)
    y_s, ka_s, va_s, kb_s, vb_s, _, _, cv_s = _trunk(
        x_sample, pos_s, w, mem_k=cache_mem_k, mem_v=cache_mem_v,
        paged=(cache_k_a, cache_v_a, cache_k_b, cache_v_b, page_table), conv_state=state_conv)

    def heads_p(xs):
        return jnp.stack([a.reshape(nb, t, heads, HEAD_DIM) for a in xs])

    def heads_s(xs):
        return jnp.stack([a.reshape(ns, ts, heads, HEAD_DIM) for a in xs])

    return (y_p, y_s,
            heads_p(ka_p), heads_p(va_p), heads_p(kb_p), heads_p(vb_p),
            jnp.stack(mk_p), jnp.stack(mv_p), jnp.stack(cv_p),
            heads_s(ka_s), heads_s(va_s), heads_s(kb_s), heads_s(vb_s), jnp.stack(cv_s))
```

```python
import functools

import jax
import jax.numpy as jnp
from jax import lax
from jax.experimental import pallas as pl
from jax.experimental.pallas import tpu as pltpu

F32 = jnp.float32
BF16 = jnp.bfloat16

HEAD_DIM = 128
ROT_DIM = HEAD_DIM // 4
ROPE_THETA = 500000.0
MOBA_BLOCK = 256
MOBA_TOPK = 3
CONV_W = 31
X_HEADS = 4
X_HEAD_DIM = 128
EPS = 1e-6
PAGE_SIZE = 128

V7X_VMEM_LIMIT_BYTES = 60 * 1024 * 1024
SUBLANES = 8
NEG_BIG = -1e30
CONV_PAD = 32


def _cparams(*sem):
    return pltpu.CompilerParams(dimension_semantics=sem, vmem_limit_bytes=V7X_VMEM_LIMIT_BYTES)


def _sigmoid(x):
    return 1.0 / (1.0 + jnp.exp(-x))


def _rmsnorm_kernel(x_ref, g_ref, o_ref):
    x = x_ref[...]
    ms = jnp.mean(x * x, axis=-1, keepdims=True)
    o_ref[...] = (x * lax.rsqrt(ms + EPS) * g_ref[...]).astype(o_ref.dtype)


def _rmsnorm(x, g, out_dtype, tm=256):
    m, d = x.shape
    tm = min(tm, m)
    return pl.pallas_call(
        _rmsnorm_kernel,
        grid=(m // tm,),
        in_specs=[pl.BlockSpec((tm, d), lambda i: (i, 0)),
                  pl.BlockSpec((1, d), lambda i: (0, 0))],
        out_specs=pl.BlockSpec((tm, d), lambda i: (i, 0)),
        out_shape=jax.ShapeDtypeStruct((m, d), out_dtype),
        compiler_params=_cparams("parallel"),
        name="rmsnorm",
    )(x, g.reshape(1, d))


def _rope_rotate(x, c, s1, s2):
    half = ROT_DIM // 2
    return x * c + pltpu.roll(x, HEAD_DIM - half, 1) * s1 + pltpu.roll(x, half, 1) * s2


def _mm_kernel(*refs, nk, n_w, has_res, has_rope, epilogue):
    a_ref = refs[0]
    w_refs = refs[1:1 + n_w]
    pos = 1 + n_w
    r_ref = refs[pos] if has_res else None
    pos += int(has_res)
    rope_refs = refs[pos:pos + 3] if has_rope else None
    pos += 3 * int(has_rope)
    o_ref = refs[pos]
    acc_refs = refs[pos + 1:]

    a = a_ref[...].astype(BF16)
    ps = [jnp.dot(a, w_ref[...].astype(BF16), preferred_element_type=F32) for w_ref in w_refs]

    def finish(vals):
        if epilogue == "swiglu":
            y = vals[0] * _sigmoid(vals[0]) * vals[1]
        elif epilogue == "glu":
            y = vals[0] * _sigmoid(vals[1])
        else:
            y = vals[0]
        if has_res:
            y = r_ref[...] + y
        if has_rope:
            c, s1, s2 = (t[...] for t in rope_refs)
            for h in range(y.shape[1] // HEAD_DIM):
                sl = slice(h * HEAD_DIM, (h + 1) * HEAD_DIM)
                o_ref[:, sl] = _rope_rotate(y[:, sl], c, s1, s2).astype(o_ref.dtype)
        else:
            o_ref[...] = y.astype(o_ref.dtype)

    if nk == 1:
        finish(ps)
    else:
        kk = pl.program_id(2)

        @pl.when(kk == 0)
        def _():
            for acc, p in zip(acc_refs, ps):
                acc[...] = p

        @pl.when(kk > 0)
        def _():
            for acc, p in zip(acc_refs, ps):
                acc[...] += p

        @pl.when(kk == nk - 1)
        def _():
            finish([acc[...] for acc in acc_refs])


def _mm(a, w, layer, *, n, k=None, col_off=0, row_off=0, w2=None, col_off2=0, epilogue="none",
        residual=None, rope=None, out_dtype=F32, tm=1024, tn=512, tk=None):
    m, ka = a.shape
    k = ka if k is None else k
    tm = min(tm, m)
    tn = min(tn, n)
    tk = k if tk is None else tk
    nk = k // tk
    assert m % tm == 0 and n % tn == 0 and k % tk == 0
    assert col_off % tn == 0 and col_off2 % tn == 0 and row_off % tk == 0
    cb, cb2, rb = col_off // tn, col_off2 // tn, row_off // tk
    ws = [w] if w2 is None else [w, w2]
    cbs = [cb, cb2]
    in_specs = [pl.BlockSpec((tm, tk), lambda i, j, kk: (i, kk))]
    for idx in range(len(ws)):
        in_specs.append(pl.BlockSpec((None, tk, tn), functools.partial(
            lambda i, j, kk, c: (layer, rb + kk, c + j), c=cbs[idx])))
    args = [a] + ws
    if residual is not None:
        in_specs.append(pl.BlockSpec((tm, tn), lambda i, j, kk: (i, j)))
        args.append(residual)
    if rope is not None:
        tables, t_len = rope
        assert t_len % tm == 0 and tn % HEAD_DIM == 0
        nt = t_len // tm
        in_specs += [pl.BlockSpec((tm, HEAD_DIM), lambda i, j, kk: (i % nt, 0))] * 3
        args += list(tables)
    scratch = [pltpu.VMEM((tm, tn), F32) for _ in ws] if nk > 1 else []
    return pl.pallas_call(
        functools.partial(_mm_kernel, nk=nk, n_w=len(ws), has_res=residual is not None,
                          has_rope=rope is not None, epilogue=epilogue),
        grid=(m // tm, n // tn, nk),
        in_specs=in_specs,
        out_specs=pl.BlockSpec((tm, tn), lambda i, j, kk: (i, j)),
        out_shape=jax.ShapeDtypeStruct((m, n), out_dtype),
        scratch_shapes=scratch,
        compiler_params=_cparams("parallel", "parallel", "arbitrary"),
        name="mm_" + epilogue,
    )(*args)


def _mm_res_norm_kernel(a_ref, w_ref, r_ref, g_ref, x_ref, h_ref):
    y = r_ref[...] + jnp.dot(a_ref[...].astype(BF16), w_ref[...].astype(BF16), preferred_element_type=F32)
    x_ref[...] = y
    ms = jnp.mean(y * y, axis=-1, keepdims=True)
    h_ref[...] = (y * lax.rsqrt(ms + EPS) * g_ref[...]).astype(h_ref.dtype)


def _mm_res_norm(a, w, layer, residual, g, h_dtype, tm=256):
    m, k = a.shape
    n = w.shape[2]
    tm = min(tm, m)
    row = lambda width: pl.BlockSpec((tm, width), lambda i: (i, 0))
    return pl.pallas_call(
        _mm_res_norm_kernel,
        grid=(m // tm,),
        in_specs=[row(k), pl.BlockSpec((None, k, n), lambda i: (layer, 0, 0)), row(n),
                  pl.BlockSpec((1, n), lambda i: (0, 0))],
        out_specs=[row(n), row(n)],
        out_shape=[jax.ShapeDtypeStruct((m, n), F32), jax.ShapeDtypeStruct((m, n), h_dtype)],
        compiler_params=_cparams("parallel"),
        name="mm_res_norm",
    )(a, w, residual, g.reshape(1, n))


def _proj_heads_kernel(a_ref, w_ref, c_ref, s1_ref, s2_ref, *o_refs, per, rope_outs):
    j = pl.program_id(1)
    y = jnp.dot(a_ref[...].astype(BF16), w_ref[...].astype(BF16), preferred_element_type=F32)
    for o, o_ref in enumerate(o_refs):
        @pl.when(j // per == o)
        def _(o=o, o_ref=o_ref):
            if o in rope_outs:
                c, s1, s2 = c_ref[...], s1_ref[...], s2_ref[...]
                for h in range(y.shape[1] // HEAD_DIM):
                    sl = slice(h * HEAD_DIM, (h + 1) * HEAD_DIM)
                    o_ref[:, sl] = _rope_rotate(y[:, sl], c, s1, s2).astype(o_ref.dtype)
            else:
                o_ref[...] = y.astype(o_ref.dtype)


def _proj_heads(a, w, layer, rope, hd, dtypes, rope_outs, tm=1024, tn=256):
    m, k = a.shape
    tables, t_len = rope
    tm = min(tm, m)
    assert m % tm == 0 and hd % tn == 0 and t_len % tm == 0 and w.shape[2] == len(dtypes) * hd
    per = hd // tn
    nt = t_len // tm
    tab = pl.BlockSpec((tm, HEAD_DIM), lambda i, j: (i % nt, 0))
    out_spec = lambda o: pl.BlockSpec((tm, tn), lambda i, j: (i, jnp.clip(j - o * per, 0, per - 1)))
    return pl.pallas_call(
        functools.partial(_proj_heads_kernel, per=per, rope_outs=rope_outs),
        grid=(m // tm, len(dtypes) * per),
        in_specs=[pl.BlockSpec((tm, k), lambda i, j: (i, 0)),
                  pl.BlockSpec((None, k, tn), lambda i, j: (layer, 0, j)), tab, tab, tab],
        out_specs=[out_spec(o) for o in range(len(dtypes))],
        out_shape=[jax.ShapeDtypeStruct((m, hd), dt) for dt in dtypes],
        compiler_params=_cparams("parallel", "arbitrary"),
        name="proj_heads",
    )(a, w, *tables)


def _rope_tables(pos):
    half = ROT_DIM // 2
    inv = ROPE_THETA ** (-jnp.arange(0, ROT_DIM, 2, dtype=F32) / ROT_DIM)
    ang = pos.astype(F32)[:, None] * inv[None, :]
    cos, sin = jnp.cos(ang), jnp.sin(ang)
    t = pos.shape[0]
    rest = HEAD_DIM - ROT_DIM
    c = jnp.concatenate([cos, cos, jnp.ones((t, rest), F32)], 1)
    s1 = jnp.concatenate([-sin, jnp.zeros((t, half + rest), F32)], 1)
    s2 = jnp.concatenate([jnp.zeros((t, half), F32), sin, jnp.zeros((t, rest), F32)], 1)
    return c, s1, s2


def _moba_tile(c, q_ref, o_ref, k16_ref, vt_ref, kmean_ref, pen_ref, *, nb):
    blk = MOBA_BLOCK
    scale = HEAD_DIM ** -0.5
    q_t = q_ref[...].T
    gate = lax.dot_general(kmean_ref[...], q_t, (((1,), (0,)), ((), ())),
                           precision=lax.Precision.HIGHEST, preferred_element_type=F32)
    bidx = lax.broadcasted_iota(jnp.int32, (nb, blk), 0)
    valid = bidx < c
    gm = jnp.where(valid, gate, -jnp.inf)
    rank = jnp.zeros((nb, blk), jnp.int32)
    for b2 in range(c):
        gb = gm[b2:b2 + 1, :]
        rank = rank + ((gb > gm) | ((gb == gm) & (b2 < bidx))).astype(jnp.int32)
    pen_ref[...] = jnp.where(valid & (rank < MOBA_TOPK), 0.0, NEG_BIG)

    q16_t = q_t.astype(BF16)
    krow = lax.broadcasted_iota(jnp.int32, (blk, blk), 0)
    qcol = lax.broadcasted_iota(jnp.int32, (blk, blk), 1)
    s_blocks = []
    for b in range(c + 1):
        s = jnp.dot(k16_ref[b * blk:(b + 1) * blk, :], q16_t, preferred_element_type=F32) * scale
        if b == c:
            s_blocks.append(jnp.where(krow <= qcol, s, NEG_BIG))
        else:
            s_blocks.append(s + pen_ref[b:b + 1, :])
    m = s_blocks[0]
    for s in s_blocks[1:]:
        m = jnp.maximum(m, s)
    m = jnp.max(m, axis=0, keepdims=True)
    p_blocks = [jnp.exp(s - m) for s in s_blocks]
    psum = p_blocks[0]
    for p in p_blocks[1:]:
        psum = psum + p
    l = jnp.sum(psum, axis=0, keepdims=True)
    p_all = jnp.concatenate([p.astype(BF16) for p in p_blocks], axis=0)
    o_t = jnp.dot(vt_ref[:, 0:(c + 1) * blk], p_all, preferred_element_type=F32)
    o_ref[...] = (o_t / l).T.astype(o_ref.dtype)


def _moba_kernel(q_ref, k_ref, v_ref, o_ref, k16_ref, vt_ref, kmean_ref, pen0_ref, pen1_ref, *, nb):
    blk = MOBA_BLOCK
    k = k_ref[...]
    kmean_ref[...] = jnp.mean(k.reshape(nb, blk, HEAD_DIM), axis=1)
    k16_ref[...] = k.astype(BF16)
    vt_ref[...] = v_ref[...].T.astype(BF16)
    for c in range(nb):
        rows = slice(c * blk, (c + 1) * blk)
        _moba_tile(c, q_ref.at[rows, :], o_ref.at[rows, :], k16_ref, vt_ref, kmean_ref,
                   pen1_ref if c % 2 else pen0_ref, nb=nb)


def _moba_prompt(q, k, v):
    b, t, w = q.shape
    h = w // HEAD_DIM
    blk = MOBA_BLOCK
    nb = max(-(-t // blk), MOBA_TOPK + 1)
    assert t % blk == 0 and nb * blk == t
    spec = pl.BlockSpec((None, t, HEAD_DIM), lambda bi, hi: (bi, 0, hi))
    return pl.pallas_call(
        functools.partial(_moba_kernel, nb=nb),
        grid=(b, h),
        in_specs=[spec, spec, spec],
        out_specs=spec,
        out_shape=jax.ShapeDtypeStruct((b, t, w), BF16),
        scratch_shapes=[pltpu.VMEM((t, HEAD_DIM), BF16), pltpu.VMEM((HEAD_DIM, t), BF16),
                        pltpu.VMEM((nb, HEAD_DIM), F32), pltpu.VMEM((nb, blk), F32),
                        pltpu.VMEM((nb, blk), F32)],
        compiler_params=_cparams("parallel", "parallel"),
        name="moba_prompt",
    )(q, k, v)


SB_BLOCK = 256


def _log_sigmoid_pair(z):
    soft = jnp.log(1.0 + jnp.exp(-jnp.abs(z)))
    ls_pos = jnp.minimum(z, 0.0) - soft
    return ls_pos, ls_pos - z


def _sb_tile(c, q_ref, o_ref, k16_ref, vt_ref, z_ref, pre_ref, hi_ref, lo_ref, a_ref, tot_ref):
    blk = SB_BLOCK
    scale = HEAD_DIM ** -0.5
    n_keys = (c + 1) * blk
    q16_t = q_ref[...].astype(F32).T.astype(BF16)
    krow = lax.broadcasted_iota(jnp.int32, (blk, blk), 0)
    qcol = lax.broadcasted_iota(jnp.int32, (blk, blk), 1)
    strictly_past = krow < qcol
    later = (qcol > krow).astype(BF16)
    blocks = [slice(kb * blk, (kb + 1) * blk) for kb in range(c + 1)]

    z_ref[0:n_keys, :] = jnp.dot(k16_ref[0:n_keys, :], q16_t, preferred_element_type=F32)
    for kb, ks in enumerate(blocks):
        ls_pos, lk = _log_sigmoid_pair(z_ref[ks, :] * scale)
        if kb == c:
            lk = jnp.where(strictly_past, lk, 0.0)
        lk_hi = lk.astype(BF16)
        hi_ref[ks, :] = lk_hi
        lo_ref[ks, :] = (lk - lk_hi.astype(F32)).astype(BF16)
        pre_ref[ks, :] = ls_pos
        tot_ref[kb:kb + 1, :] = lk[0:1, :]
    for kb, ks in enumerate(blocks):
        after = (jnp.dot(later, hi_ref[ks, :], preferred_element_type=F32)
                 + jnp.dot(later, lo_ref[ks, :], preferred_element_type=F32))
        pre_ref[ks, :] += after
        tot_ref[kb:kb + 1, :] += after[0:1, :]
    carry = jnp.zeros((1, blk), F32)
    for kb in range(c, -1, -1):
        ks = blocks[kb]
        a = jnp.exp(pre_ref[ks, :] + carry)
        if kb == c:
            a = jnp.where(strictly_past, a, 0.0)
        a_ref[ks, :] = a.astype(BF16)
        carry = carry + tot_ref[kb:kb + 1, :]
    o_t = jnp.dot(vt_ref[:, 0:n_keys], a_ref[0:n_keys, :], preferred_element_type=F32)
    o_ref[...] = o_t.T.astype(o_ref.dtype)


def _sb_kernel(q_ref, k_ref, v_ref, o_ref, k16_ref, vt_ref, *stage_refs, nq):
    blk = SB_BLOCK
    k16_ref[...] = k_ref[...].astype(BF16)
    vt_ref[...] = v_ref[...].T.astype(BF16)
    half = len(stage_refs) // 2
    for c in range(nq):
        rows = slice(c * blk, (c + 1) * blk)
        _sb_tile(c, q_ref.at[rows, :], o_ref.at[rows, :], k16_ref, vt_ref,
                 *stage_refs[(c % 2) * half:(c % 2 + 1) * half])


def _sb_prompt(q, k, v):
    b, t, w = q.shape
    h = w // HEAD_DIM
    blk = SB_BLOCK
    assert t % blk == 0
    spec = pl.BlockSpec((None, t, HEAD_DIM), lambda bi, hi: (bi, 0, hi))
    stage = [pltpu.VMEM((t, blk), F32), pltpu.VMEM((t, blk), F32),
             pltpu.VMEM((t, blk), BF16), pltpu.VMEM((t, blk), BF16),
             pltpu.VMEM((t, blk), BF16), pltpu.VMEM((t // blk, blk), F32)]
    return pl.pallas_call(
        functools.partial(_sb_kernel, nq=t // blk),
        grid=(b, h),
        in_specs=[spec, spec, spec],
        out_specs=spec,
        out_shape=jax.ShapeDtypeStruct((b, t, w), BF16),
        scratch_shapes=[pltpu.VMEM((t, HEAD_DIM), BF16), pltpu.VMEM((HEAD_DIM, t), BF16)] + stage + stage,
        compiler_params=_cparams("parallel", "parallel"),
        name="sb_prompt",
    )(q, k, v)


CONV_LANES = 256
CONV_LEAD = CONV_PAD - (CONV_W - 1)


def _layernorm_silu(y, g, b):
    mu = jnp.mean(y, axis=-1, keepdims=True)
    yc = y - mu
    var = jnp.mean(yc * yc, axis=-1, keepdims=True)
    yn = yc * lax.rsqrt(var + EPS) * g + b
    return yn * _sigmoid(yn)


def _conv_tap_groups():
    return [[CONV_LEAD + w for w in range(CONV_W) if (CONV_LEAD + w) % SUBLANES == res]
            for res in range(SUBLANES)]


def _conv_kernel(cur_ref, tail_ref, pre_ref, w_ref, bdw_ref, g_ref, b_ref, o_ref, win_ref, y_ref, sh_ref, *, tt):
    i = pl.program_id(1)
    d = cur_ref.shape[-1]

    @pl.when(i == 0)
    def _():
        win_ref[0:CONV_PAD, :] = pre_ref[...]

    @pl.when(i > 0)
    def _():
        win_ref[0:CONV_PAD, :] = tail_ref[...]

    win_ref[CONV_PAD:, :] = cur_ref[...]
    groups = _conv_tap_groups()
    for c in range(d // CONV_LANES):
        ls = slice(c * CONV_LANES, (c + 1) * CONV_LANES)
        buf = c % 2
        for res, offs in enumerate(groups):
            n = offs[-1] - offs[0] + tt
            sh_ref[buf, res, 0:n, :] = win_ref[offs[0]:offs[0] + n, ls]
        acc = jnp.zeros((tt, CONV_LANES), F32)
        for res, offs in enumerate(groups):
            for off in offs:
                w = off - CONV_LEAD
                acc = acc + w_ref[w:w + 1, ls] * sh_ref[buf, res, off - offs[0]:off - offs[0] + tt, :]
        y_ref[:, ls] = acc + bdw_ref[:, ls]
    o_ref[...] = _layernorm_silu(y_ref[...], g_ref[...], b_ref[...]).astype(o_ref.dtype)


def _conv_prompt(u, prefix, w_dw, b_dw, g_ln, b_ln, tt=128):
    b, t, d = u.shape
    assert t % tt == 0 and tt % CONV_PAD == 0 and d % CONV_LANES == 0
    per = tt // CONV_PAD
    w_pad = jnp.concatenate([w_dw, jnp.zeros((CONV_PAD - CONV_W, d), F32)], 0)
    vec = lambda x: x.reshape(1, d)
    vspec = pl.BlockSpec((1, d), lambda bi, i: (0, 0))
    shifted_rows = tt + CONV_PAD - SUBLANES
    return pl.pallas_call(
        functools.partial(_conv_kernel, tt=tt),
        grid=(b, t // tt),
        in_specs=[pl.BlockSpec((None, tt, d), lambda bi, i: (bi, i, 0)),
                  pl.BlockSpec((None, CONV_PAD, d), lambda bi, i: (bi, jnp.maximum(i * per - 1, 0), 0)),
                  pl.BlockSpec((None, CONV_PAD, d), lambda bi, i: (bi, 0, 0)),
                  pl.BlockSpec((CONV_PAD, d), lambda bi, i: (0, 0)),
                  vspec, vspec, vspec],
        out_specs=pl.BlockSpec((None, tt, d), lambda bi, i: (bi, i, 0)),
        out_shape=jax.ShapeDtypeStruct((b, t, d), BF16),
        scratch_shapes=[pltpu.VMEM((tt + CONV_PAD, d), F32), pltpu.VMEM((tt, d), F32),
                        pltpu.VMEM((2, SUBLANES, shifted_rows, CONV_LANES), F32)],
        compiler_params=_cparams("parallel", "arbitrary"),
        name="conv_prompt",
    )(u, u, prefix, w_pad, vec(b_dw), vec(g_ln), vec(b_ln))


def _conv_sample_kernel(st_ref, u_ref, w_ref, bdw_ref, g_ref, b_ref, o_ref):
    n = st_ref.shape[0]
    for s in range(n):
        st = st_ref[s]
        y = jnp.sum(st * w_ref[0:CONV_W - 1, :], axis=0, keepdims=True)
        y = y + u_ref[s:s + 1, :] * w_ref[CONV_W - 1:CONV_W, :] + bdw_ref[...]
        o_ref[s:s + 1, :] = _layernorm_silu(y, g_ref[...], b_ref[...]).astype(o_ref.dtype)


def _conv_sample(state, u, w_dw, b_dw, g_ln, b_ln):
    n, _, d = state.shape
    vec = lambda x: x.reshape(1, d)
    return pl.pallas_call(
        _conv_sample_kernel,
        out_shape=jax.ShapeDtypeStruct((n, d), F32),
        compiler_params=pltpu.CompilerParams(vmem_limit_bytes=V7X_VMEM_LIMIT_BYTES),
        name="conv_sample",
    )(state, u, w_dw, vec(b_dw), vec(g_ln), vec(b_ln))


def _xattn_kernel(q_ref, k_ref, v_ref, o_ref, *, tq):
    scale = X_HEAD_DIM ** -0.5
    for h in range(X_HEADS):
        sl = slice(h * X_HEAD_DIM, (h + 1) * X_HEAD_DIM)
        if tq == 1:
            q = q_ref[:, sl].astype(F32)
            s = jnp.sum(k_ref[:, sl] * q, axis=-1, keepdims=True) * scale
            p = jnp.exp(s - jnp.max(s, axis=0, keepdims=True))
            p = p / jnp.sum(p, axis=0, keepdims=True)
            o = jnp.sum(p * v_ref[:, sl], axis=0, keepdims=True)
        else:
            q = q_ref[:, sl].astype(BF16)
            s = lax.dot_general(q, k_ref[:, sl].astype(BF16), (((1,), (1,)), ((), ())),
                                preferred_element_type=F32) * scale
            p = jnp.exp(s - jnp.max(s, axis=1, keepdims=True))
            p = p / jnp.sum(p, axis=1, keepdims=True)
            o = jnp.dot(p.astype(BF16), v_ref[:, sl].astype(BF16), preferred_element_type=F32)
        o_ref[:, sl] = o.astype(o_ref.dtype)


def _xattn(q, mk, mv, layer, out_dtype, tq=512):
    n, t, xw = q.shape
    mem = mk.shape[2]
    tq = min(tq, t)
    kv_spec = pl.BlockSpec((None, None, mem, xw), lambda ni, i: (layer, ni, 0, 0))
    q_spec = pl.BlockSpec((None, tq, xw), lambda ni, i: (ni, i, 0))
    return pl.pallas_call(
        functools.partial(_xattn_kernel, tq=tq),
        grid=(n, t // tq),
        in_specs=[q_spec, kv_spec, kv_spec],
        out_specs=q_spec,
        out_shape=jax.ShapeDtypeStruct((n, t, xw), out_dtype),
        compiler_params=_cparams("parallel", "parallel"),
        name="xattn",
    )(q, mk, mv)


SELECT_BLOCKS_PER_STEP = 2


def _moba_select_kernel(pt_ref, *refs, nblk, pages_per_blk):
    del pt_ref
    n_pages = SELECT_BLOCKS_PER_STEP * pages_per_blk
    page_refs, (q_ref, sel_ref, km_ref) = refs[:n_pages], refs[n_pages:]
    step = pl.program_id(1)
    for sb in range(SELECT_BLOCKS_PER_STEP):
        total = jnp.sum(page_refs[sb * pages_per_blk][...], axis=0)
        for pg in range(1, pages_per_blk):
            total = total + jnp.sum(page_refs[sb * pages_per_blk + pg][...], axis=0)
        km_ref[step * SELECT_BLOCKS_PER_STEP + sb] = total * (1.0 / MOBA_BLOCK)

    @pl.when(step == nblk // SELECT_BLOCKS_PER_STEP - 1)
    def _():
        gate = jnp.sum(km_ref[...] * q_ref[...][None], axis=-1, keepdims=True)
        bidx = lax.broadcasted_iota(jnp.int32, gate.shape, 0)
        rank = jnp.zeros(gate.shape, jnp.int32)
        for b2 in range(nblk):
            gb = gate[b2:b2 + 1]
            rank = rank + ((gb > gate) | ((gb == gate) & (b2 < bidx))).astype(jnp.int32)
        for s in range(MOBA_TOPK):
            sel_ref[s] = jnp.sum(jnp.where(rank == s, bidx, 0), axis=0)


def _moba_sample_select(cache_k, layer, page_table, q):
    n, h, dh = q.shape
    n_pages = page_table.shape[1]
    pages_per_blk = MOBA_BLOCK // PAGE_SIZE
    nblk = n_pages // pages_per_blk
    assert nblk >= MOBA_TOPK and nblk % SELECT_BLOCKS_PER_STEP == 0
    pages_per_step = SELECT_BLOCKS_PER_STEP * pages_per_blk
    page_spec = lambda which: pl.BlockSpec(
        (None, None, PAGE_SIZE, h, dh),
        lambda ni, b, pt: (layer, pt[ni, pages_per_step * b + which], 0, 0, 0))
    sel = pl.pallas_call(
        functools.partial(_moba_select_kernel, nblk=nblk, pages_per_blk=pages_per_blk),
        grid_spec=pltpu.PrefetchScalarGridSpec(
            num_scalar_prefetch=1,
            grid=(n, nblk // SELECT_BLOCKS_PER_STEP),
            in_specs=[page_spec(p) for p in range(pages_per_step)]
                     + [pl.BlockSpec((None, h, dh), lambda ni, b, pt: (ni, 0, 0))],
            out_specs=pl.BlockSpec((None, MOBA_TOPK, h, 1), lambda ni, b, pt: (ni, 0, 0, 0)),
            scratch_shapes=[pltpu.VMEM((nblk, h, dh), F32)]),
        out_shape=jax.ShapeDtypeStruct((n, MOBA_TOPK, h, 1), jnp.int32),
        compiler_params=_cparams("parallel", "arbitrary"),
        name="moba_sample_select",
    )(page_table, *([cache_k] * pages_per_step), q)
    return sel.reshape(n, MOBA_TOPK, h)


def _moba_attend_kernel(pt_ref, sel_ref, ck_hbm, cv_hbm, q_ref, kn_ref, vn_ref, o_ref,
                        kbuf, vbuf, sem, *, layer, heads):
    ni = pl.program_id(0)
    pages_per_blk = MOBA_BLOCK // PAGE_SIZE
    n_slices = MOBA_TOPK * pages_per_blk
    scale = HEAD_DIM ** -0.5

    def slice_copies(h, j):
        blk_id = sel_ref[ni, (j // pages_per_blk) * heads + h]
        page = pt_ref[ni, blk_id * pages_per_blk + j % pages_per_blk]
        rows = pl.ds(j * PAGE_SIZE, PAGE_SIZE)
        return (pltpu.make_async_copy(ck_hbm.at[layer, page, :, h, :], kbuf.at[h, rows, :], sem.at[0]),
                pltpu.make_async_copy(cv_hbm.at[layer, page, :, h, :], vbuf.at[h, rows, :], sem.at[1]))

    for h in range(heads):
        for j in range(n_slices):
            for cp in slice_copies(h, j):
                cp.start()
    for h in range(heads):
        for j in range(n_slices):
            for cp in slice_copies(h, j):
                cp.wait()

    for h in range(heads):
        q = q_ref[h:h + 1, :]
        s_new = jnp.sum(q * kn_ref[h:h + 1, :], axis=-1, keepdims=True) * scale
        s = jnp.sum(kbuf[h] * q, axis=-1, keepdims=True) * scale
        m = jnp.maximum(jnp.max(s, axis=0, keepdims=True), s_new)
        p = jnp.exp(s - m)
        p_new = jnp.exp(s_new - m)
        l = jnp.sum(p, axis=0, keepdims=True) + p_new
        o = jnp.sum(p * vbuf[h], axis=0, keepdims=True) + p_new * vn_ref[h:h + 1, :]
        o_ref[h:h + 1, :] = o / l


def _moba_sample_attend(cache_k, cache_v, layer, page_table, sel, q, k_new, v_new):
    n, h, dh = q.shape
    n_keys = MOBA_TOPK * MOBA_BLOCK
    vec_spec = pl.BlockSpec((None, h, dh), lambda ni, pt, sl: (ni, 0, 0))
    any_spec = pl.BlockSpec(memory_space=pl.ANY)
    return pl.pallas_call(
        functools.partial(_moba_attend_kernel, layer=layer, heads=h),
        grid_spec=pltpu.PrefetchScalarGridSpec(
            num_scalar_prefetch=2,
            grid=(n,),
            in_specs=[any_spec, any_spec, vec_spec, vec_spec, vec_spec],
            out_specs=vec_spec,
            scratch_shapes=[pltpu.VMEM((h, n_keys, dh), F32), pltpu.VMEM((h, n_keys, dh), F32),
                            pltpu.SemaphoreType.DMA((2,))]),
        out_shape=jax.ShapeDtypeStruct((n, h, dh), F32),
        compiler_params=_cparams("arbitrary"),
        name="moba_sample_attend",
    )(page_table, sel.reshape(n, MOBA_TOPK * h), cache_k, cache_v, q, k_new, v_new)


def _sb_sample_kernel(pt_ref, k_ref, v_ref, q_ref, o_ref, c_ref, acc_ref, *, n_pages):
    del pt_ref
    j = pl.program_id(1)
    scale = HEAD_DIM ** -0.5

    @pl.when(j == 0)
    def _():
        c_ref[...] = jnp.zeros(c_ref.shape, F32)
        acc_ref[...] = jnp.zeros(acc_ref.shape, F32)

    z = jnp.sum(k_ref[...] * q_ref[...][None], axis=-1, keepdims=True) * scale
    ls_pos, lk = _log_sigmoid_pair(z)
    incl = lk
    shift = 1
    while shift < PAGE_SIZE:
        incl = incl + jnp.concatenate(
            [incl[shift:], jnp.zeros((shift,) + incl.shape[1:], F32)], axis=0)
        shift *= 2
    a = jnp.exp(ls_pos + (incl - lk) + c_ref[...][None])
    acc_ref[...] += jnp.sum(a * v_ref[...], axis=0)
    c_ref[...] += incl[0]

    @pl.when(j == n_pages - 1)
    def _():
        o_ref[...] = acc_ref[...]


def _sb_sample(cache_k, cache_v, layer, page_table, q):
    n, h, dh = q.shape
    n_pages = page_table.shape[1]
    page_spec = pl.BlockSpec((None, None, PAGE_SIZE, h, dh),
                             lambda ni, j, pt: (layer, pt[ni, n_pages - 1 - j], 0, 0, 0))
    vec_spec = pl.BlockSpec((None, h, dh), lambda ni, j, pt: (ni, 0, 0))
    return pl.pallas_call(
        functools.partial(_sb_sample_kernel, n_pages=n_pages),
        grid_spec=pltpu.PrefetchScalarGridSpec(
            num_scalar_prefetch=1,
            grid=(n, n_pages),
            in_specs=[page_spec, page_spec, vec_spec],
            out_specs=vec_spec,
            scratch_shapes=[pltpu.VMEM((h, 1), F32), pltpu.VMEM((h, dh), F32)]),
        out_shape=jax.ShapeDtypeStruct((n, h, dh), F32),
        compiler_params=_cparams("parallel", "arbitrary"),
        name="sb_sample",
    )(page_table, cache_k, cache_v, q)


def _trunk(x3, pos, w, *, mem=None, mem_k=None, mem_v=None, paged=None, conv_state=None):
    n, t, d = x3.shape
    m = n * t
    depth = w["g_mix"].shape[0]
    hd = w["w_out"].shape[1] // 2
    xw = w["w_xq"].shape[2]
    d_ff = w["w_ffn_gate"].shape[2]
    act = BF16 if t > 1 else F32
    dual = dict(tm=1024, tn=256)
    wide = dict(tm=2048, tn=256)
    tables = _rope_tables(pos)
    if t == 1:
        tables = tuple(jnp.broadcast_to(tb, (n, HEAD_DIM)) for tb in tables)
    x = x3.reshape(m, d)
    nka, nva, nkb, nvb, nmk, nmv, nconv = [], [], [], [], [], [], []
    for l in range(depth):
        h = _rmsnorm(x, w["g_mix"][l], act)
        if l % 2 == 0:
            i = l // 2
            rope = (tables, t if t > 1 else n)
            qa, ka, va, qb, kb, vb = _proj_heads(h, w["w_in"], i, rope, hd,
                                                 (F32, F32, F32, act, F32, F32), rope_outs=(0, 1))
            nka.append(ka); nva.append(va); nkb.append(kb); nvb.append(vb)
            if paged is None:
                r3 = lambda a: a.reshape(n, t, hd)
                oa = _moba_prompt(r3(qa), r3(ka), r3(va)).reshape(m, hd)
                ob = _sb_prompt(r3(qb), r3(kb), r3(vb)).reshape(m, hd)
            else:
                ck_a, cv_a, ck_b, cv_b, table = paged
                heads = hd // HEAD_DIM
                rh = lambda a: a.reshape(n, heads, HEAD_DIM)
                sel = _moba_sample_select(ck_a, i, table, rh(qa))
                oa = _moba_sample_attend(ck_a, cv_a, i, table, sel, rh(qa), rh(ka), rh(va)).reshape(m, hd)
                ob = _sb_sample(ck_b, cv_b, i, table, rh(qb)).reshape(m, hd)
            oab = jnp.concatenate([oa, ob], -1)
            x = _mm(oab, w["w_out"], i, n=d, residual=x, **wide)
        else:
            j = l // 2
            u = _mm(h, w["w_pw1"], j, w2=w["w_pw1"], n=d, col_off=0, col_off2=d, epilogue="glu", **dual)
            if conv_state is None:
                prefix = jnp.zeros((n, CONV_PAD, d), F32)
                y = _conv_prompt(u.reshape(n, t, d), prefix, w["w_dw"][j], w["b_dw"][j],
                                 w["g_conv_ln"][j], w["b_conv_ln"][j]).reshape(m, d)
                nconv.append(u.reshape(n, t, d)[:, t - (CONV_W - 1):])
            else:
                st = conv_state[j]
                y = _conv_sample(st, u, w["w_dw"][j], w["b_dw"][j], w["g_conv_ln"][j], w["b_conv_ln"][j])
                nconv.append(jnp.concatenate([st, u[:, None, :]], 1)[:, 1:])
            x = _mm(y, w["w_pw2"], j, n=d, residual=x, **wide)
        if mem is not None:
            mn = _rmsnorm(mem.reshape(-1, d), w["g_mem"][l], BF16)
            mk = _mm(mn, w["w_xkv"], l, n=xw, col_off=0)
            mv = _mm(mn, w["w_xkv"], l, n=xw, col_off=xw)
            mem_len = mem.shape[1]
            nmk.append(mk.reshape(n, mem_len, X_HEADS, X_HEAD_DIM))
            nmv.append(mv.reshape(n, mem_len, X_HEADS, X_HEAD_DIM))
            mk4, mv4, lyr = mk.reshape(1, n, mem_len, xw), mv.reshape(1, n, mem_len, xw), 0
        else:
            mk4 = mem_k.reshape(mem_k.shape[0], n, mem_k.shape[2], xw)
            mv4 = mem_v.reshape(mem_v.shape[0], n, mem_v.shape[2], xw)
            lyr = l
        hx = _rmsnorm(x, w["g_xattn"][l], act)
        q = _mm(hx, w["w_xq"], l, n=xw, out_dtype=act)
        o = _xattn(q.reshape(n, t, xw), mk4, mv4, lyr, act).reshape(m, xw)
        x, hf = _mm_res_norm(o, w["w_xo"], l, x, w["g_ffn"][l], act)
        g = _mm(hf, w["w_ffn_gate"], l, w2=w["w_ffn_up"], n=d_ff, epilogue="swiglu", out_dtype=act, **dual)
        x = _mm(g, w["w_ffn_down_bf16"], l, n=d, residual=x, tk=d_ff // 2)
    y = _rmsnorm(x, w["g_final"], F32).reshape(n, t, d)
    return y, nka, nva, nkb, nvb, nmk, nmv, nconv


def kernel(x_prompt, x_sample, cache_k_a, cache_v_a, cache_k_b, cache_v_b, cache_mem_k, cache_mem_v,
           state_conv, page_table, mem_prompt, g_mix, w_in, w_out, w_pw1, w_dw, b_dw, g_conv_ln,
           b_conv_ln, w_pw2, g_mem, g_xattn, w_xq, w_xkv, w_xo, g_ffn, w_ffn_gate, w_ffn_up,
           w_ffn_down, g_final):
    w = {"g_mix": g_mix, "w_in": w_in, "w_out": w_out, "w_pw1": w_pw1, "w_dw": w_dw,
         "b_dw": b_dw, "g_conv_ln": g_conv_ln, "b_conv_ln": b_conv_ln, "w_pw2": w_pw2,
         "g_mem": g_mem, "g_xattn": g_xattn, "w_xq": w_xq, "w_xkv": w_xkv,
         "w_xo": w_xo, "g_ffn": g_ffn, "w_ffn_gate": w_ffn_gate, "w_ffn_up": w_ffn_up,
         "w_ffn_down_bf16": w_ffn_down.astype(BF16), "g_final": g_final}
    nb, t, _ = x_prompt.shape
    ns, ts, _ = x_sample.shape
    past_len = page_table.shape[1] * PAGE_SIZE
    pos_p = jnp.arange(t, dtype=jnp.int32)
    pos_s = past_len + jnp.arange(ts, dtype=jnp.int32)
    heads = w_out.shape[1] // 2 // HEAD_DIM

    y_p, ka_p, va_p, kb_p, vb_p, mk_p, mv_p, cv_p = _trunk(x_prompt, pos_p, w, mem=mem_prompt)
    y_s, ka_s, va_s, kb_s, vb_s, _, _, cv_s = _trunk(
        x_sample, pos_s, w, mem_k=cache_mem_k, mem_v=cache_mem_v,
        paged=(cache_k_a, cache_v_a, cache_k_b, cache_v_b, page_table), conv_state=state_conv)

    def heads_p(xs):
        return jnp.stack([a.reshape(nb, t, heads, HEAD_DIM) for a in xs])

    def heads_s(xs):
        return jnp.stack([a.reshape(ns, ts, heads, HEAD_DIM) for a in xs])

    return (y_p, y_s,
            heads_p(ka_p), heads_p(va_p), heads_p(kb_p), heads_p(vb_p),
            jnp.stack(mk_p), jnp.stack(mv_p), jnp.stack(cv_p),
            heads_s(ka_s), heads_s(va_s), heads_s(kb_s), heads_s(vb_s), jnp.stack(cv_s))
```
